```python
import jax, jax.numpy as jnp
from jax import lax
import numpy as np


D_MODEL = 2048
BATCH = 4
SEQ = 2048
DEPTH = 2

D_MIX = D_MODEL
RWKV_W = 3 * D_MIX // 8
RET_W = D_MIX // 4
GDN_W = D_MIX - RWKV_W - RET_W
RWKV_HEAD_DIM = 64
RET_HEAD_DIM = 128
GDN_HEAD_DIM = 128
RWKV_HEADS = RWKV_W // RWKV_HEAD_DIM
RET_HEADS = RET_W // RET_HEAD_DIM
GDN_HEADS = GDN_W // GDN_HEAD_DIM
RWKV_DECAY_RANK = 96
RWKV_A_RANK = 96
RWKV_VRES_RANK = 64
RWKV_DECAY_SCALE = 0.6065306597126334
RWKV_GN_EPS = 64e-5
RET_CHUNK = 128
RET_GN_EPS = 1e-5
ROPE_BASE = 10000.0
GDN_CONV = 4
GDN_CHUNK = 64
MOE_GROUPS = 4
MOE_EXPERTS_PER_GROUP = 8
MOE_EXPERTS = MOE_GROUPS * MOE_EXPERTS_PER_GROUP
MOE_TOP_K = 2
MOE_HIDDEN = D_MODEL // 8
NORM_EPS = 1e-6
IN_SPLITS = (RWKV_W,) * 4 + (RET_W,) * 4 + (GDN_W,) * 4 + (GDN_HEADS,) * 2
N_IN = sum(IN_SPLITS)

kernel_name = "hybrid_rwkv7_retnet_gdn_hmoe"


def _rms_norm(x, w, eps=NORM_EPS):
    xf = x.astype(jnp.float32)
    y = xf * lax.rsqrt(jnp.mean(xf * xf, axis=-1, keepdims=True) + eps)
    return (y * w.astype(jnp.float32)).astype(x.dtype)


def _token_shift(x):
    pad = [(0, 0)] * x.ndim
    pad[1] = (1, 0)
    return jnp.pad(x[:, :-1], pad)


def _split_heads(t, n_heads):
    return t.reshape(*t.shape[:-1], n_heads, t.shape[-1] // n_heads)


def _l2norm(t, eps=1e-6):
    return t * lax.rsqrt(jnp.sum(t * t, axis=-1, keepdims=True) + eps)


def _head_norm(t, eps):
    mu = jnp.mean(t, axis=-1, keepdims=True)
    var = jnp.mean(jnp.square(t - mu), axis=-1, keepdims=True)
    y = (t - mu) * lax.rsqrt(var + eps)
    return y.reshape(*t.shape[:-2], -1)


def _to_chunks(t, c):
    b, s = t.shape[:2]
    return t.reshape(b, s // c, c, *t.shape[2:]).swapaxes(2, 3)


def _from_chunks(t):
    t = t.swapaxes(2, 3)
    return t.reshape(t.shape[0], -1, *t.shape[3:])


def _rwkv7_recurrence(r, w, k, v, kk, a):
    b, s, h, n = r.shape

    def step(state, inp):
        r_t, w_t, k_t, v_t, kk_t, a_t = inp
        sa = jnp.einsum('bhvk,bhk->bhv', state, -kk_t)
        state = (state * w_t[:, :, None, :]
                 + sa[..., None] * (kk_t * a_t)[:, :, None, :]
                 + v_t[..., None] * k_t[:, :, None, :])
        return state, jnp.einsum('bhvk,bhk->bhv', state, r_t)

    xs = tuple(jnp.moveaxis(t, 1, 0) for t in (r, w, k, v, kk, a))
    _, o = lax.scan(step, jnp.zeros((b, h, n, n), jnp.float32), xs)
    return jnp.moveaxis(o, 0, 1)


def _rwkv7_mix(h, r, k, v, g, mu_rkvg, mu_wa, w0, w1, w2, a0, a1, a2, k_k, k_a, r_k,
               ln_w, ln_b, v_first, vres):
    f32 = jnp.float32
    p = jnp.stack([r, k, v, g], axis=2).astype(f32)
    p = p + (_token_shift(p) - p) * mu_rkvg
    r, k, v, g = p[:, :, 0], p[:, :, 1], p[:, :, 2], p[:, :, 3]
    hf = h.astype(f32)
    dh = _token_shift(hf) - hf
    xw = hf + dh * mu_wa[0]
    xa = hf + dh * mu_wa[1]
    log_w = -RWKV_DECAY_SCALE * jax.nn.sigmoid(w0 + jnp.tanh(xw @ w1) @ w2)
    a = jax.nn.sigmoid(a0 + (xa @ a1) @ a2)
    if vres is None:
        v_first = v
    else:
        mu_v, v0, v1, v2 = vres
        xv = hf + dh * mu_v
        v = v + (v_first - v) * jax.nn.sigmoid(v0 + (xv @ v1) @ v2)
    kk = _l2norm(_split_heads(k * k_k, RWKV_HEADS))
    k = k * (1.0 + (a - 1.0) * k_a)
    r_h, k_h, v_h, a_h, w_h = (_split_heads(t, RWKV_HEADS) for t in (r, k, v, a, jnp.exp(log_w)))
    o = _rwkv7_recurrence(r_h, w_h, k_h, v_h, kk, a_h)
    o = _head_norm(o, RWKV_GN_EPS) * ln_w + ln_b
    bonus = (jnp.sum(r_h * k_h * r_k, axis=-1, keepdims=True) * v_h).reshape(o.shape)
    return (o + bonus) * jax.nn.sigmoid(g), v_first


def _rotary(t, positions):
    d = t.shape[-1]
    inv_freq = ROPE_BASE ** (-jnp.arange(0, d, 2, dtype=jnp.float32) / d)
    ang = positions.astype(jnp.float32)[..., None] * inv_freq
    cos = jnp.cos(ang)[:, :, None, :]
    sin = jnp.sin(ang)[:, :, None, :]
    t1, t2 = t[..., : d // 2], t[..., d // 2:]
    return jnp.concatenate([t1 * cos - t2 * sin, t1 * sin + t2 * cos], axis=-1)


def _retention_chunked(q, k, v, log_gamma):
    c = RET_CHUNK
    b, _, h, dk = q.shape
    dv = v.shape[-1]
    q, k, v = (_to_chunks(t, c) for t in (q, k, v))
    idx = jnp.arange(c, dtype=jnp.float32)
    diff = idx[:, None] - idx[None, :]
    dmask = jnp.where(diff >= 0, jnp.exp(jnp.maximum(diff, 0.0) * log_gamma[:, None, None]), 0.0)
    scores = jnp.einsum('bnhid,bnhjd->bnhij', q, k) * dmask
    inner = jnp.einsum('bnhij,bnhjv->bnhiv', scores, v)
    kdec = k * jnp.exp((c - 1 - idx) * log_gamma[:, None])[:, :, None]
    kv = jnp.einsum('bnhjk,bnhjv->bnhkv', kdec, v)
    chunk_decay = jnp.exp(c * log_gamma)[:, None, None]

    def step(state, kv_n):
        return state * chunk_decay + kv_n, state

    _, r_prev = lax.scan(step, jnp.zeros((b, h, dk, dv), jnp.float32), jnp.moveaxis(kv, 1, 0))
    r_prev = jnp.moveaxis(r_prev, 0, 1)
    qdec = q * jnp.exp((idx + 1.0) * log_gamma[:, None])[:, :, None]
    cross = jnp.einsum('bnhik,bnhkv->bnhiv', qdec, r_prev)
    return _from_chunks(inner + cross)


def _retnet_mix(q, k, v, g, positions, gn_w):
    f32 = jnp.float32
    q = _rotary(_split_heads(q.astype(f32), RET_HEADS), positions)
    k = _rotary(_split_heads(k.astype(f32), RET_HEADS), positions) * RET_HEAD_DIM ** -0.5
    v = _split_heads(v.astype(f32), RET_HEADS)
    log_gamma = jnp.log(1.0 - 2.0 ** (-5.0 - jnp.arange(RET_HEADS, dtype=f32)))
    y = _retention_chunked(q, k, v, log_gamma)
    y = _head_norm(y, RET_GN_EPS) * gn_w
    return jax.nn.silu(g.astype(f32)) * y


def _causal_depthwise_conv(x, w):
    kw, c = w.shape
    return lax.conv_general_dilated(x, w[:, None, :].astype(x.dtype), window_strides=(1,),
                                    padding=[(kw - 1, 0)], dimension_numbers=('NWC', 'WIO', 'NWC'),
                                    feature_group_count=c)


def _gated_delta_chunked(q, k, v, g, beta):
    c = GDN_CHUNK
    q, k, v, g, beta = (_to_chunks(t, c) for t in (q, k, v, g, beta))
    b, _, h, _, dk = q.shape
    dv = v.shape[-1]
    gc = jnp.cumsum(g, axis=-1)
    causal = jnp.tril(jnp.ones((c, c), bool))
    strict = jnp.tril(jnp.ones((c, c), bool), -1)
    decay = jnp.exp(jnp.where(causal, gc[..., :, None] - gc[..., None, :], -jnp.inf))
    kb = k * beta[..., None]
    lower = jnp.where(strict, jnp.einsum('bnhid,bnhjd->bnhij', kb, k) * decay, 0.0)
    rhs = jnp.concatenate([v * beta[..., None], kb * jnp.exp(gc)[..., None]], axis=-1)
    sol = lax.linalg.triangular_solve(lower + jnp.eye(c, dtype=lower.dtype), rhs,
                                      left_side=True, lower=True, unit_diagonal=True)
    u, w = sol[..., :dv], sol[..., dv:]
    attn = jnp.einsum('bnhid,bnhjd->bnhij', q, k) * decay
    g_last = gc[..., -1]
    qg = q * jnp.exp(gc)[..., None]
    kd = k * jnp.exp(g_last[..., None] - gc)[..., None]

    def step(state, inp):
        qg_n, kd_n, u_n, w_n, attn_n, gl_n = inp
        v_new = u_n - jnp.einsum('bhck,bhkv->bhcv', w_n, state)
        o = jnp.einsum('bhck,bhkv->bhcv', qg_n, state) + jnp.einsum('bhij,bhjv->bhiv', attn_n, v_new)
        state = state * jnp.exp(gl_n)[..., None, None] + jnp.einsum('bhck,bhcv->bhkv', kd_n, v_new)
        return state, o

    xs = tuple(jnp.moveaxis(t, 1, 0) for t in (qg, kd, u, w, attn, g_last))
    _, o = lax.scan(step, jnp.zeros((b, h, dk, dv), jnp.float32), xs)
    return _from_chunks(jnp.moveaxis(o, 0, 1))


def _gdn_mix(q, k, v, z, a, b, conv_w, a_log, dt_bias, norm_w):
    f32 = jnp.float32
    qkv = jnp.concatenate([q, k, v], axis=-1).astype(f32)
    qkv = jax.nn.silu(_causal_depthwise_conv(qkv, conv_w.astype(f32)))
    q, k, v = jnp.split(qkv, 3, axis=-1)
    q = _l2norm(_split_heads(q, GDN_HEADS)) * GDN_HEAD_DIM ** -0.5
    k = _l2norm(_split_heads(k, GDN_HEADS))
    v = _split_heads(v, GDN_HEADS)
    beta = jax.nn.sigmoid(b.astype(f32))
    g = -jnp.exp(a_log.astype(f32)) * jax.nn.softplus(a.astype(f32) + dt_bias)
    o = _gated_delta_chunked(q, k, v, g, beta)
    o = (o * lax.rsqrt(jnp.mean(o * o, axis=-1, keepdims=True) + NORM_EPS)).reshape(*o.shape[:-2], -1)
    return o * norm_w * jax.nn.silu(z.astype(f32))


def _hier_moe(x, group_w, group_b, expert_w, expert_b, w1, w3, w2):
    f32 = jnp.float32
    b, s, d = x.shape
    xt = x.reshape(b * s, d)
    group_logits = (xt @ group_w + group_b).astype(f32)
    group_probs = jax.nn.softmax(group_logits, axis=-1)
    group_idx = jnp.argmax(group_logits, axis=-1)
    group_gate = jnp.take_along_axis(group_probs, group_idx[:, None], axis=-1)
    expert_logits = (xt @ expert_w + expert_b).astype(f32).reshape(-1, MOE_GROUPS, MOE_EXPERTS_PER_GROUP)
    in_group = jnp.take_along_axis(expert_logits, group_idx[:, None, None], axis=1)[:, 0]
    top_p, top_i = lax.top_k(jax.nn.softmax(in_group, axis=-1), MOE_TOP_K)
    top_p = top_p / jnp.sum(top_p, axis=-1, keepdims=True)
    within = jnp.sum(top_p[..., None] * jax.nn.one_hot(top_i, MOE_EXPERTS_PER_GROUP, dtype=f32), axis=-2)
    gates = (group_gate * jax.nn.one_hot(group_idx, MOE_GROUPS, dtype=f32))[:, :, None] * within[:, None, :]
    gates = gates.reshape(-1, MOE_EXPERTS).astype(x.dtype)
    hid = jax.nn.silu(jnp.einsum('td,edf->tef', xt, w1)) * jnp.einsum('td,edf->tef', xt, w3)
    y = jnp.einsum('tef,efd->td', hid * gates[..., None], w2)
    return y.reshape(b, s, d)


def setup_inputs(seed: int = 0) -> dict:
    key = jax.random.key(seed)
    keys = iter(jax.random.split(key, 40))
    f32 = jnp.float32

    def nrm(shape, scale):
        return scale * jax.random.normal(next(keys), shape, f32)

    def uni(shape, lo, hi):
        return jax.random.uniform(next(keys), shape, f32, lo, hi)

    L, Lv, D = DEPTH, DEPTH - 1, D_MODEL
    x = nrm((BATCH, SEQ, D), 1.0)
    offset = jax.random.randint(next(keys), (BATCH, 1), 0, 4096, dtype=jnp.int32)
    positions = (offset + jnp.arange(SEQ, dtype=jnp.int32)[None, :]).astype(jnp.int32)
    dt = jnp.exp(uni((L, GDN_HEADS), float(np.log(1e-3)), float(np.log(1e-1))))
    gdn_dt_bias = dt + jnp.log(-jnp.expm1(-dt))
    gdn_a_log = jnp.log(uni((L, GDN_HEADS), 1.0, 16.0))
    return {
        'x': x,
        'positions': positions,
        'norm1_w': 1.0 + nrm((L, D), 0.02),
        'w_in': nrm((L, D, N_IN), D ** -0.5),
        'w_out': nrm((L, D_MIX, D), D_MIX ** -0.5),
        'rwkv_mu_rkvg': uni((L, 4, RWKV_W), 0.0, 1.0),
        'rwkv_mu_wa': uni((L, 2, D), 0.0, 1.0),
        'rwkv_w0': uni((L, RWKV_W), -2.0, 2.0),
        'rwkv_w1': nrm((L, D, RWKV_DECAY_RANK), D ** -0.5),
        'rwkv_w2': nrm((L, RWKV_DECAY_RANK, RWKV_W), 0.5 * RWKV_DECAY_RANK ** -0.5),
        'rwkv_a0': nrm((L, RWKV_W), 0.5),
        'rwkv_a1': nrm((L, D, RWKV_A_RANK), D ** -0.5),
        'rwkv_a2': nrm((L, RWKV_A_RANK, RWKV_W), 0.5 * RWKV_A_RANK ** -0.5),
        'rwkv_k_k': uni((L, RWKV_W), 0.7, 1.0),
        'rwkv_k_a': uni((L, RWKV_W), 0.8, 1.2),
        'rwkv_r_k': nrm((L, RWKV_HEADS, RWKV_HEAD_DIM), 0.1),
        'rwkv_ln_w': 1.0 + nrm((L, RWKV_W), 0.02),
        'rwkv_ln_b': nrm((L, RWKV_W), 0.02),
        'rwkv_mu_vres': uni((Lv, D), 0.0, 1.0),
        'rwkv_v0': nrm((Lv, RWKV_W), 0.5),
        'rwkv_v1': nrm((Lv, D, RWKV_VRES_RANK), D ** -0.5),
        'rwkv_v2': nrm((Lv, RWKV_VRES_RANK, RWKV_W), 0.5 * RWKV_VRES_RANK ** -0.5),
        'ret_gn_w': 1.0 + nrm((L, RET_W), 0.02),
        'gdn_conv_w': nrm((L, GDN_CONV, 3 * GDN_W), GDN_CONV ** -0.5),
        'gdn_a_log': gdn_a_log,
        'gdn_dt_bias': gdn_dt_bias,
        'gdn_norm_w': 1.0 + nrm((L, GDN_W), 0.02),
        'norm2_w': 1.0 + nrm((L, D), 0.02),
        'moe_group_w': nrm((L, D, MOE_GROUPS), D ** -0.5),
        'moe_group_b': nrm((L, MOE_GROUPS), 0.01),
        'moe_expert_w': nrm((L, D, MOE_EXPERTS), D ** -0.5),
        'moe_expert_b': nrm((L, MOE_EXPERTS), 0.01),
        'moe_w1': nrm((L, MOE_EXPERTS, D, MOE_HIDDEN), D ** -0.5),
        'moe_w3': nrm((L, MOE_EXPERTS, D, MOE_HIDDEN), D ** -0.5),
        'moe_w2': nrm((L, MOE_EXPERTS, MOE_HIDDEN, D), MOE_HIDDEN ** -0.5),
        'final_norm_w': 1.0 + nrm((D,), 0.02),
    }


def reference(x, positions, norm1_w, w_in, w_out,
              rwkv_mu_rkvg, rwkv_mu_wa, rwkv_w0, rwkv_w1, rwkv_w2, rwkv_a0, rwkv_a1, rwkv_a2,
              rwkv_k_k, rwkv_k_a, rwkv_r_k, rwkv_ln_w, rwkv_ln_b,
              rwkv_mu_vres, rwkv_v0, rwkv_v1, rwkv_v2,
              ret_gn_w, gdn_conv_w, gdn_a_log, gdn_dt_bias, gdn_norm_w,
              norm2_w, moe_group_w, moe_group_b, moe_expert_w, moe_expert_b,
              moe_w1, moe_w3, moe_w2, final_norm_w):
    split_points = np.cumsum(IN_SPLITS)[:-1].tolist()
    v_first = None
    for l in range(DEPTH):
        h = _rms_norm(x, norm1_w[l])
        (rr, rk, rv, rg, tq, tk, tv, tg, dq, dk, dv, dz, da, db) = jnp.split(h @ w_in[l], split_points, axis=-1)
        vres = None if l == 0 else (rwkv_mu_vres[l - 1], rwkv_v0[l - 1], rwkv_v1[l - 1], rwkv_v2[l - 1])
        out_a, v_first = _rwkv7_mix(h, rr, rk, rv, rg, rwkv_mu_rkvg[l], rwkv_mu_wa[l], rwkv_w0[l],
                                    rwkv_w1[l], rwkv_w2[l], rwkv_a0[l], rwkv_a1[l], rwkv_a2[l],
                                    rwkv_k_k[l], rwkv_k_a[l], rwkv_r_k[l], rwkv_ln_w[l], rwkv_ln_b[l],
                                    v_first, vres)
        out_b = _retnet_mix(tq, tk, tv, tg, positions, ret_gn_w[l])
        out_c = _gdn_mix(dq, dk, dv, dz, da, db, gdn_conv_w[l], gdn_a_log[l], gdn_dt_bias[l], gdn_norm_w[l])
        mixed = jnp.concatenate([out_a, out_b, out_c], axis=-1).astype(x.dtype)
        x = x + mixed @ w_out[l]
        x = x + _hier_moe(_rms_norm(x, norm2_w[l]), moe_group_w[l], moe_group_b[l], moe_expert_w[l],
                          moe_expert_b[l], moe_w1[l], moe_w3[l], moe_w2[l])
    return _rms_norm(x, final_norm_w)
```

```python
import functools
import math

import jax
import jax.numpy as jnp
from jax import lax
from jax.experimental import pallas as pl
from jax.experimental.pallas import tpu as pltpu

F32 = jnp.float32
BF16 = jnp.bfloat16

D_MODEL = 2048
RWKV_W = 768
RET_W = 512
GDN_W = 768
RWKV_HEAD = 64
RWKV_PAIRS = RWKV_W // 128
RET_HEADS = RET_W // 128
GDN_HEADS = GDN_W // 128
LANES = 128
RWKV_DECAY_SCALE = 0.6065306597126334
RWKV_GN_EPS = 64e-5
RET_GN_EPS = 1e-5
ROPE_BASE = 10000.0
GDN_CONV = 4
MOE_GROUPS = 4
MOE_PER_GROUP = 8
MOE_EXPERTS = 32
MOE_HIDDEN = 256
NORM_EPS = 1e-6
L2_EPS = 1e-6
N_MAIN = 4 * RWKV_W + 4 * RET_W + 4 * GDN_W
N_EXT = 7 * LANES
CHUNK = 64
RET_CHUNK = 128
VMEM_LIMIT = 56 * 1024 * 1024


def _cparams(sem):
    return pltpu.CompilerParams(dimension_semantics=sem, vmem_limit_bytes=VMEM_LIMIT)


def _dot(a, b):
    return jnp.dot(a.astype(BF16), b.astype(BF16), preferred_element_type=F32)


def _dot_nt(a, b):
    return lax.dot_general(a.astype(BF16), b.astype(BF16), (((1,), (1,)), ((), ())),
                           preferred_element_type=F32)


def _dot_tn(a, b):
    return lax.dot_general(a.astype(BF16), b.astype(BF16), (((0,), (0,)), ((), ())),
                           preferred_element_type=F32)


def _sigmoid(x):
    return 1.0 / (1.0 + jnp.exp(-x))


def _silu(x):
    return x * _sigmoid(x)


def _shift_rows(prev8, x, n):
    xs = jnp.concatenate([prev8, x], axis=0)
    return pltpu.roll(xs, n, axis=0)[8:]


def _cumsum_rows(x, tril):
    hi = x.astype(BF16)
    lo = (x - hi.astype(F32)).astype(BF16)
    return (jnp.dot(tril, hi, preferred_element_type=F32)
            + jnp.dot(tril, lo, preferred_element_type=F32))


def _unit_lower_inverse(low, eye, expand, steps):
    x = eye + low
    p = low
    for _ in range(steps):
        p = _dot(p, expand(p))
        x = x + _dot(x, expand(p))
    return x


def _rmsnorm_kernel(x_ref, w_ref, o_ref):
    x = x_ref[...]
    y = x * lax.rsqrt(jnp.mean(x * x, axis=-1, keepdims=True) + NORM_EPS) * w_ref[...]
    o_ref[...] = y.astype(o_ref.dtype)


def _rmsnorm(x, w, out_dtype, tm=512):
    t, d = x.shape
    return pl.pallas_call(
        _rmsnorm_kernel,
        grid=(t // tm,),
        in_specs=[pl.BlockSpec((tm, d), lambda i: (i, 0)), pl.BlockSpec((1, d), lambda i: (0, 0))],
        out_specs=pl.BlockSpec((tm, d), lambda i: (i, 0)),
        out_shape=jax.ShapeDtypeStruct((t, d), out_dtype),
        compiler_params=_cparams(("parallel",)),
        name="rmsnorm",
    )(x, w.reshape(1, d))


def _matmul_kernel(a_ref, b_ref, o_ref):
    o_ref[...] = jnp.dot(a_ref[...], b_ref[...], preferred_element_type=F32).astype(o_ref.dtype)


def _matmul(a, b, tm, tn, out_dtype=F32):
    m, k = a.shape
    n = b.shape[1]
    return pl.pallas_call(
        _matmul_kernel,
        grid=(n // tn, m // tm),
        in_specs=[pl.BlockSpec((tm, k), lambda j, i: (i, 0)), pl.BlockSpec((k, tn), lambda j, i: (0, j))],
        out_specs=pl.BlockSpec((tm, tn), lambda j, i: (i, j)),
        out_shape=jax.ShapeDtypeStruct((m, n), out_dtype),
        compiler_params=_cparams(("parallel", "parallel")),
        name="in_proj",
    )(a, b)


def _outproj_kernel(a_ref, b_ref, c_ref, wa_ref, wb_ref, wc_ref, x_ref, o_ref):
    acc = x_ref[...]
    acc = acc + jnp.dot(a_ref[...], wa_ref[...], preferred_element_type=F32)
    acc = acc + jnp.dot(b_ref[...], wb_ref[...], preferred_element_type=F32)
    acc = acc + jnp.dot(c_ref[...], wc_ref[...], preferred_element_type=F32)
    o_ref[...] = acc


def _outproj(oa, ob, oc, w_out, x, tm=512):
    t, d = x.shape
    wa = w_out[:RWKV_W]
    wb = w_out[RWKV_W:RWKV_W + RET_W]
    wc = w_out[RWKV_W + RET_W:]
    row = lambda i: (i, 0)
    const = lambda i: (0, 0)
    return pl.pallas_call(
        _outproj_kernel,
        grid=(t // tm,),
        in_specs=[pl.BlockSpec((tm, RWKV_W), row), pl.BlockSpec((tm, RET_W), row),
                  pl.BlockSpec((tm, GDN_W), row),
                  pl.BlockSpec((RWKV_W, d), const), pl.BlockSpec((RET_W, d), const),
                  pl.BlockSpec((GDN_W, d), const), pl.BlockSpec((tm, d), row)],
        out_specs=pl.BlockSpec((tm, d), row),
        out_shape=jax.ShapeDtypeStruct((t, d), F32),
        compiler_params=_cparams(("parallel",)),
        name="out_proj",
    )(oa, ob, oc, wa, wb, wc, x)


def _rope_kernel(pos_ref, inv_ref, cos_ref, sin_ref):
    ang = pos_ref[...].astype(F32) * inv_ref[...]
    lane = lax.broadcasted_iota(jnp.int32, ang.shape, 1)
    s = jnp.sin(ang)
    cos_ref[...] = jnp.cos(ang)
    sin_ref[...] = jnp.where(lane < LANES // 2, -s, s)


def _rope_tables(positions, tm=512):
    t = positions.size
    half = LANES // 2
    inv = ROPE_BASE ** (-jnp.arange(0, LANES, 2, dtype=F32) / LANES)
    inv = jnp.concatenate([inv, inv]).reshape(1, LANES)
    tm = min(tm, t)
    del half
    return pl.pallas_call(
        _rope_kernel,
        grid=(t // tm,),
        in_specs=[pl.BlockSpec((tm, 1), lambda i: (i, 0)), pl.BlockSpec((1, LANES), lambda i: (0, 0))],
        out_specs=[pl.BlockSpec((tm, LANES), lambda i: (i, 0))] * 2,
        out_shape=[jax.ShapeDtypeStruct((t, LANES), F32)] * 2,
        compiler_params=_cparams(("parallel",)),
        name="rope_tables",
    )(positions.reshape(t, 1), inv)


def _rwkv_kernel(*refs, tt, has_vres):
    if has_vres:
        (r_ref, k_ref, v_ref, g_ref, ext_ref, mu_ref, par_ref, w2_ref, a2_ref, v2_ref, vf_ref,
         out_ref, state_ref, carry_ref, ecarry_ref) = refs
    else:
        (r_ref, k_ref, v_ref, g_ref, ext_ref, mu_ref, par_ref, w2_ref, a2_ref,
         out_ref, vf_out_ref, state_ref, carry_ref, ecarry_ref) = refs
    c = CHUNK
    hd = RWKV_HEAD

    @pl.when(pl.program_id(2) == 0)
    def _():
        state_ref[...] = jnp.zeros_like(state_ref)
        carry_ref[...] = jnp.zeros_like(carry_ref)
        ecarry_ref[...] = jnp.zeros_like(ecarry_ref)

    lane = lax.broadcasted_iota(jnp.int32, (1, LANES), 1)
    first = lane < hd

    def head_sum(x):
        s1 = jnp.sum(jnp.where(first, x, 0.0), axis=-1, keepdims=True)
        s2 = jnp.sum(jnp.where(first, 0.0, x), axis=-1, keepdims=True)
        return jnp.where(first, s1, s2)

    def expand(x):
        return jnp.concatenate([jnp.where(first, x, 0.0), jnp.where(first, 0.0, x)], axis=0)

    mu = mu_ref[...]
    par = par_ref[...]
    w0, a0, v0, k_k, k_a, ln_w, ln_b, r_k = (par[i:i + 1] for i in range(8))

    def mixed(ref, i):
        x = ref[...]
        xs = _shift_rows(carry_ref[i], x, 1)
        carry_ref[i] = x[tt - 8:]
        return x + (xs - x) * mu[i:i + 1]

    r = mixed(r_ref, 0)
    k = mixed(k_ref, 1)
    v = mixed(v_ref, 2)
    g = mixed(g_ref, 3)

    ext = ext_ref[...]
    exts = _shift_rows(ecarry_ref[...], ext, 1)
    ecarry_ref[...] = ext[tt - 8:]

    def low_rank(i):
        return ext[:, 2 * i * LANES:(2 * i + 1) * LANES] + exts[:, (2 * i + 1) * LANES:(2 * i + 2) * LANES]

    log_w = -RWKV_DECAY_SCALE * _sigmoid(w0 + _dot(jnp.tanh(low_rank(0)), w2_ref[...]))
    a = _sigmoid(a0 + _dot(low_rank(1), a2_ref[...]))
    if has_vres:
        v = v + (vf_ref[...] - v) * _sigmoid(v0 + _dot(low_rank(2), v2_ref[...]))
    else:
        vf_out_ref[...] = v

    kk = k * k_k
    kk = kk * lax.rsqrt(head_sum(kk * kk) + L2_EPS)
    k = k * (1.0 + (a - 1.0) * k_a)
    bonus = head_sum(r * k * r_k) * v
    kka = kk * a

    row = lax.broadcasted_iota(jnp.int32, (c, 2 * c), 0)
    col = lax.broadcasted_iota(jnp.int32, (c, 2 * c), 1) & (c - 1)
    strict = col < row
    incl = col <= row
    eye = jnp.where(col == row, 1.0, 0.0)
    tri_r = lax.broadcasted_iota(jnp.int32, (c, c), 0)
    tri_c = lax.broadcasted_iota(jnp.int32, (c, c), 1)
    tril = jnp.where(tri_c <= tri_r, 1.0, 0.0).astype(BF16)
    srow = lax.broadcasted_iota(jnp.int32, (LANES, LANES), 0)
    scol = lax.broadcasted_iota(jnp.int32, (LANES, LANES), 1)
    same_head = (srow < hd) == (scol < hd)

    state = state_ref[...]
    for ci in range(tt // c):
        sl = slice(ci * c, (ci + 1) * c)
        lw = log_w[sl]
        gc = _cumsum_rows(lw, tril)
        g_last = gc[c - 1:c]
        dec_in = jnp.exp(gc)
        dec_ex = jnp.exp(gc - lw)
        dec_inv = jnp.exp(-gc)
        dec_out = jnp.exp(g_last - gc)
        a_t = -kk[sl] * dec_ex
        r_t = r[sl] * dec_in
        b_t = kka[sl] * dec_inv
        k_t = k[sl] * dec_inv
        vc = v[sl]
        sc = _dot_nt(jnp.concatenate([a_t, r_t], axis=0),
                     jnp.concatenate([expand(b_t), expand(k_t)], axis=0))
        l_ab = jnp.where(strict, sc[:c, :2 * c], 0.0)
        l_ak = jnp.where(strict, sc[:c, 2 * c:], 0.0)
        m_rb = jnp.where(incl, sc[c:, :2 * c], 0.0)
        m_rk = jnp.where(incl, sc[c:, 2 * c:], 0.0)
        t_inv = _unit_lower_inverse(l_ab, eye, expand, 5)
        rhs = _dot_nt(a_t, state) + _dot(l_ak, expand(vc))
        u = _dot(t_inv, expand(rhs))
        o = _dot_nt(r_t, state) + _dot(m_rb, expand(u)) + _dot(m_rk, expand(vc))
        upd = _dot_tn(jnp.concatenate([u, vc], axis=0),
                      jnp.concatenate([kka[sl] * dec_out, k[sl] * dec_out], axis=0))
        state = state * jnp.exp(g_last) + jnp.where(same_head, upd, 0.0)

        mean = head_sum(o) * (1.0 / hd)
        cen = o - mean
        var = head_sum(cen * cen) * (1.0 / hd)
        y = cen * lax.rsqrt(var + RWKV_GN_EPS) * ln_w + ln_b
        out_ref[sl, :] = ((y + bonus[sl]) * _sigmoid(g[sl])).astype(out_ref.dtype)
    state_ref[...] = state


def _rwkv_mix(pm, pe, mu_rkvg, par, w2, a2, v2, v_first, batch, tt):
    t = pm.shape[0]
    nt = t // batch // tt
    has_vres = v_first is not None
    nb = RWKV_PAIRS

    def col(off):
        return pl.BlockSpec((tt, LANES), lambda b, p, i: (b * nt + i, off + p))

    pcol = lambda rows: pl.BlockSpec((rows, LANES), lambda b, p, i: (0, p))
    in_specs = [col(0), col(nb), col(2 * nb), col(3 * nb),
                pl.BlockSpec((tt, 6 * LANES), lambda b, p, i: (b * nt + i, 0)),
                pcol(4), pcol(8), pcol(LANES), pcol(LANES)]
    args = [pm, pm, pm, pm, pe, mu_rkvg, par, w2, a2]
    out_block = pl.BlockSpec((tt, LANES), lambda b, p, i: (b * nt + i, p))
    if has_vres:
        in_specs += [pcol(LANES), out_block]
        args += [v2, v_first]
        out_specs = out_block
        out_shape = jax.ShapeDtypeStruct((t, RWKV_W), BF16)
    else:
        out_specs = [out_block, out_block]
        out_shape = [jax.ShapeDtypeStruct((t, RWKV_W), BF16), jax.ShapeDtypeStruct((t, RWKV_W), F32)]
    res = pl.pallas_call(
        functools.partial(_rwkv_kernel, tt=tt, has_vres=has_vres),
        grid=(batch, nb, nt),
        in_specs=in_specs,
        out_specs=out_specs,
        out_shape=out_shape,
        scratch_shapes=[pltpu.VMEM((LANES, LANES), F32), pltpu.VMEM((4, 8, LANES), F32),
                        pltpu.VMEM((8, 6 * LANES), F32)],
        compiler_params=_cparams(("parallel", "parallel", "arbitrary")),
        name="rwkv7_vres" if has_vres else "rwkv7",
    )(*args)
    if has_vres:
        return res, v_first
    return res[0], res[1]


def _ret_kernel(q_ref, k_ref, v_ref, g_ref, cos_ref, sin_ref, gn_ref, out_ref, state_ref, *, tt):
    c = RET_CHUNK

    @pl.when(pl.program_id(2) == 0)
    def _():
        state_ref[...] = jnp.zeros_like(state_ref)

    head = pl.program_id(1).astype(F32)
    log_gamma = jnp.log(1.0 - jnp.exp2(jnp.full((1, 1), -5.0, F32) - head))
    cos = cos_ref[...]
    sin = sin_ref[...]

    def rotary(x):
        return x * cos + pltpu.roll(x, LANES // 2, axis=1) * sin

    q = rotary(q_ref[...])
    k = rotary(k_ref[...]) * (LANES ** -0.5)
    v = v_ref[...]
    ri = lax.broadcasted_iota(jnp.int32, (c, c), 0)
    ci_ = lax.broadcasted_iota(jnp.int32, (c, c), 1)
    diff = (ri - ci_).astype(F32)
    dmask = jnp.where(diff >= 0, jnp.exp(jnp.maximum(diff, 0.0) * log_gamma), 0.0)
    idx = lax.broadcasted_iota(jnp.int32, (c, 1), 0).astype(F32)
    k_dec = jnp.exp((c - 1.0 - idx) * log_gamma)
    q_dec = jnp.exp((idx + 1.0) * log_gamma)
    chunk_decay = jnp.exp(c * log_gamma)

    state = state_ref[...]
    for i in range(tt // c):
        sl = slice(i * c, (i + 1) * c)
        qc, kc, vc = q[sl], k[sl], v[sl]
        scores = _dot_nt(qc, kc) * dmask
        y = _dot(scores, vc) + _dot(qc * q_dec, state)
        state = state * chunk_decay + _dot_tn(kc * k_dec, vc)
        mean = jnp.mean(y, axis=-1, keepdims=True)
        cen = y - mean
        var = jnp.mean(cen * cen, axis=-1, keepdims=True)
        yn = cen * lax.rsqrt(var + RET_GN_EPS) * gn_ref[...]
        out_ref[sl, :] = (_silu(g_ref[sl, :]) * yn).astype(out_ref.dtype)
    state_ref[...] = state


def _ret_mix(pm, cos_t, sin_t, gn_w, batch, tt):
    t = pm.shape[0]
    nt = t // batch // tt
    base = 4 * RWKV_PAIRS

    def col(off):
        return pl.BlockSpec((tt, LANES), lambda b, h, i: (b * nt + i, base + off + h))

    tab = pl.BlockSpec((tt, LANES), lambda b, h, i: (b * nt + i, 0))
    return pl.pallas_call(
        functools.partial(_ret_kernel, tt=tt),
        grid=(batch, RET_HEADS, nt),
        in_specs=[col(0), col(RET_HEADS), col(2 * RET_HEADS), col(3 * RET_HEADS), tab, tab,
                  pl.BlockSpec((1, LANES), lambda b, h, i: (0, h))],
        out_specs=pl.BlockSpec((tt, LANES), lambda b, h, i: (b * nt + i, h)),
        out_shape=jax.ShapeDtypeStruct((t, RET_W), BF16),
        scratch_shapes=[pltpu.VMEM((LANES, LANES), F32)],
        compiler_params=_cparams(("parallel", "parallel", "arbitrary")),
        name="retention",
    )(pm, pm, pm, pm, cos_t, sin_t, gn_w.reshape(1, RET_W))


def _gdn_kernel(q_ref, k_ref, v_ref, z_ref, ab_ref, abt_ref, cw_ref, par_ref, nw_ref,
                out_ref, state_ref, carry_ref, *, tt):
    c = CHUNK

    @pl.when(pl.program_id(2) == 0)
    def _():
        state_ref[...] = jnp.zeros_like(state_ref)
        carry_ref[...] = jnp.zeros_like(carry_ref)

    cw = cw_ref[...]

    def conv_silu(ref, i):
        x = ref[...]
        prev = carry_ref[i]
        acc = x * cw[i, GDN_CONV - 1:GDN_CONV]
        for j in range(GDN_CONV - 1):
            acc = acc + _shift_rows(prev, x, GDN_CONV - 1 - j) * cw[i, j:j + 1]
        carry_ref[i] = x[tt - 8:]
        return _silu(acc)

    def l2norm(x):
        return x * lax.rsqrt(jnp.sum(x * x, axis=-1, keepdims=True) + L2_EPS)

    q = l2norm(conv_silu(q_ref, 0)) * (LANES ** -0.5)
    k = l2norm(conv_silu(k_ref, 1))
    v = conv_silu(v_ref, 2)

    par = par_ref[...]
    neg_rate = -jnp.exp(par[:, 0:1])
    dt_bias = par[:, 1:2]

    def softplus(x):
        return jnp.maximum(x, 0.0) + jnp.log(1.0 + jnp.exp(-jnp.abs(x)))

    ab = ab_ref[...]
    abt = abt_ref[...]
    g_col = neg_rate * softplus(ab[:, 0:1] + dt_bias)
    g_row = neg_rate * softplus(abt[0:1, :] + dt_bias)
    beta = _sigmoid(ab[:, 1:2])

    ri = lax.broadcasted_iota(jnp.int32, (c, c), 0)
    ci_ = lax.broadcasted_iota(jnp.int32, (c, c), 1)
    causal = ci_ <= ri
    strict = ci_ < ri
    eye = jnp.where(ci_ == ri, 1.0, 0.0)
    ident = lambda x: x

    state = state_ref[...]
    for i in range(tt // c):
        sl = slice(i * c, (i + 1) * c)
        gr = g_row[:, sl]
        gcl = g_col[sl]
        gc_col = jnp.sum(jnp.where(causal, gr, 0.0), axis=1, keepdims=True)
        gc_row = jnp.sum(jnp.where(ri <= ci_, gcl, 0.0), axis=0, keepdims=True)
        g_last = gc_col[c - 1:c]
        decay = jnp.where(causal, jnp.exp(jnp.minimum(gc_col - gc_row, 0.0)), 0.0)
        qc, kc, vc, bc = q[sl], k[sl], v[sl], beta[sl]
        kb = kc * bc
        kk_t = _dot_nt(jnp.concatenate([kb, qc], axis=0), kc)
        lower = jnp.where(strict, kk_t[:c] * decay, 0.0)
        attn = kk_t[c:] * decay
        t_inv = _unit_lower_inverse(-lower, eye, ident, 5)
        uw = _dot(t_inv, jnp.concatenate([vc * bc, kb * jnp.exp(gc_col)], axis=1))
        v_new = uw[:, :LANES] - _dot(uw[:, LANES:], state)
        o = _dot(qc * jnp.exp(gc_col), state) + _dot(attn, v_new)
        state = state * jnp.exp(g_last) + _dot_tn(kc * jnp.exp(g_last - gc_col), v_new)
        on = o * lax.rsqrt(jnp.mean(o * o, axis=-1, keepdims=True) + NORM_EPS)
        out_ref[sl, :] = (on * nw_ref[...] * _silu(z_ref[sl, :])).astype(out_ref.dtype)
    state_ref[...] = state


def _gdn_mix(pm, ab, abt, conv_w, par, norm_w, batch, tt):
    t = pm.shape[0]
    nt = t // batch // tt
    base = 4 * RWKV_PAIRS + 4 * RET_HEADS

    def col(off):
        return pl.BlockSpec((tt, LANES), lambda b, h, i: (b * nt + i, base + off + h))

    return pl.pallas_call(
        functools.partial(_gdn_kernel, tt=tt),
        grid=(batch, GDN_HEADS, nt),
        in_specs=[col(0), col(GDN_HEADS), col(2 * GDN_HEADS), col(3 * GDN_HEADS),
                  pl.BlockSpec((None, tt, 2), lambda b, h, i: (h, b * nt + i, 0)),
                  pl.BlockSpec((None, 2, tt), lambda b, h, i: (h, 0, b * nt + i)),
                  pl.BlockSpec((3, GDN_CONV, LANES), lambda b, h, i: (0, 0, h)),
                  pl.BlockSpec((None, 1, LANES), lambda b, h, i: (h, 0, 0)),
                  pl.BlockSpec((1, LANES), lambda b, h, i: (0, h))],
        out_specs=pl.BlockSpec((tt, LANES), lambda b, h, i: (b * nt + i, h)),
        out_shape=jax.ShapeDtypeStruct((t, GDN_W), BF16),
        scratch_shapes=[pltpu.VMEM((LANES, LANES), F32), pltpu.VMEM((3, 8, LANES), F32)],
        compiler_params=_cparams(("parallel", "parallel", "arbitrary")),
        name="gated_deltanet",
    )(pm, pm, pm, pm, ab, abt, conv_w, par, norm_w.reshape(1, GDN_W))


def _router_kernel(x_ref, nw_ref, wr_ref, br_ref, h_ref, gates_ref):
    x = x_ref[...]
    h = x * lax.rsqrt(jnp.mean(x * x, axis=-1, keepdims=True) + NORM_EPS) * nw_ref[...]
    h_ref[...] = h.astype(h_ref.dtype)
    logits = jnp.dot(h, wr_ref[...], precision=lax.Precision.HIGHEST,
                     preferred_element_type=F32) + br_ref[...]
    lane_i = lax.broadcasted_iota(jnp.int32, logits.shape, 1)
    lane = lane_i.astype(F32)
    group_of_lane = (lane_i // MOE_PER_GROUP).astype(F32)
    neg = -jnp.inf
    is_group = (lane_i >= MOE_EXPERTS) & (lane_i < MOE_EXPERTS + MOE_GROUPS)
    gl = jnp.where(is_group, logits, neg)
    gmax = jnp.max(gl, axis=-1, keepdims=True)
    gidx = jnp.min(jnp.where(gl == gmax, lane - MOE_EXPERTS, LANES), axis=-1, keepdims=True)
    group_gate = 1.0 / jnp.sum(jnp.exp(gl - gmax), axis=-1, keepdims=True)
    in_group = (lane_i < MOE_EXPERTS) & (group_of_lane == gidx)
    el = jnp.where(in_group, logits, neg)
    m1 = jnp.max(el, axis=-1, keepdims=True)
    i1 = jnp.min(jnp.where(el == m1, lane, LANES), axis=-1, keepdims=True)
    el2 = jnp.where(lane == i1, neg, el)
    m2 = jnp.max(el2, axis=-1, keepdims=True)
    i2 = jnp.min(jnp.where(el2 == m2, lane, LANES), axis=-1, keepdims=True)
    esum = jnp.sum(jnp.exp(el - m1), axis=-1, keepdims=True)
    p1 = 1.0 / esum
    p2 = jnp.exp(m2 - m1) / esum
    tot = p1 + p2
    within = jnp.where(lane == i1, p1 / tot, jnp.where(lane == i2, p2 / tot, 0.0))
    gates_ref[...] = group_gate * within


def _router(x, norm_w, wr, br, tm=512):
    t, d = x.shape
    row = lambda i: (i, 0)
    const = lambda i: (0, 0)
    return pl.pallas_call(
        _router_kernel,
        grid=(t // tm,),
        in_specs=[pl.BlockSpec((tm, d), row), pl.BlockSpec((1, d), const),
                  pl.BlockSpec((d, LANES), const), pl.BlockSpec((1, LANES), const)],
        out_specs=[pl.BlockSpec((tm, d), row), pl.BlockSpec((tm, LANES), row)],
        out_shape=[jax.ShapeDtypeStruct((t, d), BF16), jax.ShapeDtypeStruct((t, LANES), F32)],
        compiler_params=_cparams(("parallel",)),
        name="moe_router",
    )(x, norm_w.reshape(1, d), wr, br)


def _moe_dense_kernel(h_ref, g_ref, w1_ref, w3_ref, w2_ref, x_ref, o_ref, acc_ref, *, te):
    j = pl.program_id(1)

    @pl.when(j == 0)
    def _():
        acc_ref[...] = x_ref[...]

    h = h_ref[...]
    gts = g_ref[...]
    acc = acc_ref[...]
    for e in range(te):
        a1 = jnp.dot(h, w1_ref[e], preferred_element_type=F32)
        a3 = jnp.dot(h, w3_ref[e], preferred_element_type=F32)
        hid = _silu(a1) * a3 * gts[:, e:e + 1]
        acc = acc + jnp.dot(hid.astype(BF16), w2_ref[e], preferred_element_type=F32)
    acc_ref[...] = acc

    @pl.when(j == pl.num_programs(1) - 1)
    def _():
        o_ref[...] = acc


def _moe_dense(h, gates, w1, w3, w2, x, tm=512, te=2):
    t, d = x.shape
    ne = MOE_EXPERTS // te
    g = gates[:, :MOE_EXPERTS].reshape(t, ne, te).transpose(1, 0, 2)
    return pl.pallas_call(
        functools.partial(_moe_dense_kernel, te=te),
        grid=(t // tm, ne),
        in_specs=[pl.BlockSpec((tm, d), lambda i, j: (i, 0)),
                  pl.BlockSpec((None, tm, te), lambda i, j: (j, i, 0)),
                  pl.BlockSpec((te, d, MOE_HIDDEN), lambda i, j: (j, 0, 0)),
                  pl.BlockSpec((te, d, MOE_HIDDEN), lambda i, j: (j, 0, 0)),
                  pl.BlockSpec((te, MOE_HIDDEN, d), lambda i, j: (j, 0, 0)),
                  pl.BlockSpec((tm, d), lambda i, j: (i, 0))],
        out_specs=pl.BlockSpec((tm, d), lambda i, j: (i, 0)),
        out_shape=jax.ShapeDtypeStruct((t, d), F32),
        scratch_shapes=[pltpu.VMEM((tm, d), F32)],
        compiler_params=_cparams(("parallel", "arbitrary")),
        name="moe_experts",
    )(h, g, w1, w3, w2, x)


def _pad_cols(w, n):
    return jnp.pad(w, ((0, 0), (0, n - w.shape[1])))


def _pad_rows(w, n):
    return jnp.pad(w, ((0, n - w.shape[0]), (0, 0)))


def _mixers(l, h, batch, tt, cos_t, sin_t, v_first, w_in, rwkv_mu_rkvg, rwkv_mu_wa, rwkv_w0, rwkv_w1,
            rwkv_w2, rwkv_a0, rwkv_a1, rwkv_a2, rwkv_k_k, rwkv_k_a, rwkv_r_k, rwkv_ln_w, rwkv_ln_b,
            rwkv_mu_vres, rwkv_v0, rwkv_v1, rwkv_v2, ret_gn_w, gdn_conv_w, gdn_a_log, gdn_dt_bias,
            gdn_norm_w):
    t = h.shape[0]
    d = D_MODEL
    w_main = w_in[l][:, :N_MAIN].astype(BF16)
    mu_w = rwkv_mu_wa[l][0][:, None]
    mu_a = rwkv_mu_wa[l][1][:, None]
    blocks = [(1.0 - mu_w) * rwkv_w1[l], mu_w * rwkv_w1[l], (1.0 - mu_a) * rwkv_a1[l], mu_a * rwkv_a1[l]]
    if l > 0:
        mu_v = rwkv_mu_vres[l - 1][:, None]
        blocks += [(1.0 - mu_v) * rwkv_v1[l - 1], mu_v * rwkv_v1[l - 1]]
        v0 = rwkv_v0[l - 1]
        v2 = _pad_rows(rwkv_v2[l - 1], LANES).astype(BF16)
    else:
        blocks += [jnp.zeros((d, LANES), F32)] * 2
        v0 = jnp.zeros((RWKV_W,), F32)
        v2 = None
    blocks.append(w_in[l][:, N_MAIN:])
    w_ext = jnp.concatenate([_pad_cols(b, LANES) for b in blocks], axis=1).astype(BF16)

    pm = _matmul(h, w_main, tm=min(1024, t), tn=1024)
    pe = _matmul(h, w_ext, tm=min(1024, t), tn=N_EXT)

    par = jnp.stack([rwkv_w0[l], rwkv_a0[l], v0, rwkv_k_k[l], rwkv_k_a[l], rwkv_ln_w[l], rwkv_ln_b[l],
                     rwkv_r_k[l].reshape(RWKV_W)])
    out_a, v_first = _rwkv_mix(pm, pe, rwkv_mu_rkvg[l], par,
                               _pad_rows(rwkv_w2[l], LANES).astype(BF16),
                               _pad_rows(rwkv_a2[l], LANES).astype(BF16), v2, v_first, batch, tt)
    out_b = _ret_mix(pm, cos_t, sin_t, ret_gn_w[l], batch, tt)

    ab = pe[:, 6 * LANES:6 * LANES + 2 * GDN_HEADS].reshape(t, 2, GDN_HEADS)
    gpar = _pad_cols(jnp.stack([gdn_a_log[l], gdn_dt_bias[l]], axis=1), LANES).reshape(GDN_HEADS, 1, LANES)
    out_c = _gdn_mix(pm, ab.transpose(2, 0, 1), ab.transpose(2, 1, 0),
                     gdn_conv_w[l].reshape(GDN_CONV, 3, GDN_W).transpose(1, 0, 2), gpar,
                     gdn_norm_w[l], batch, tt)
    return out_a, out_b, out_c, v_first


def kernel(x, positions, norm1_w, w_in, w_out, rwkv_mu_rkvg, rwkv_mu_wa, rwkv_w0, rwkv_w1, rwkv_w2, rwkv_a0, rwkv_a1, rwkv_a2, rwkv_k_k, rwkv_k_a, rwkv_r_k, rwkv_ln_w, rwkv_ln_b, rwkv_mu_vres, rwkv_v0, rwkv_v1, rwkv_v2, ret_gn_w, gdn_conv_w, gdn_a_log, gdn_dt_bias, gdn_norm_w, norm2_w, moe_group_w, moe_group_b, moe_expert_w, moe_expert_b, moe_w1, moe_w3, moe_w2, final_norm_w):
    batch, seq, d = x.shape
    t = batch * seq
    depth = w_in.shape[0]
    tt = min(256, seq)
    tm = min(512, t)
    xf = x.reshape(t, d)
    cos_t, sin_t = _rope_tables(positions, tm)
    v_first = None
    for l in range(depth):
        h = _rmsnorm(xf, norm1_w[l], BF16, tm)
        out_a, out_b, out_c, v_first = _mixers(
            l, h, batch, tt, cos_t, sin_t, v_first, w_in, rwkv_mu_rkvg, rwkv_mu_wa, rwkv_w0, rwkv_w1,
            rwkv_w2, rwkv_a0, rwkv_a1, rwkv_a2, rwkv_k_k, rwkv_k_a, rwkv_r_k, rwkv_ln_w, rwkv_ln_b,
            rwkv_mu_vres, rwkv_v0, rwkv_v1, rwkv_v2, ret_gn_w, gdn_conv_w, gdn_a_log, gdn_dt_bias,
            gdn_norm_w)
        xf = _outproj(out_a, out_b, out_c, w_out[l].astype(BF16), xf, tm)
        wr = _pad_cols(jnp.concatenate([moe_expert_w[l], moe_group_w[l]], axis=1), LANES)
        br = _pad_cols(jnp.concatenate([moe_expert_b[l], moe_group_b[l]]).reshape(1, -1), LANES)
        h2, gates = _router(xf, norm2_w[l], wr, br, tm)
        xf = _moe_dense(h2, gates, moe_w1[l].astype(BF16), moe_w3[l].astype(BF16),
                        moe_w2[l].astype(BF16), xf, tm)
    out = _rmsnorm(xf, final_norm_w, F32, tm)
    return out.reshape(batch, seq, d)
```

```python
import functools

import jax
import jax.numpy as jnp
from jax import lax
from jax.experimental import pallas as pl
from jax.experimental.pallas import tpu as pltpu

F32 = jnp.float32
BF16 = jnp.bfloat16

D_MODEL = 2048
RWKV_W = 768
RET_W = 512
GDN_W = 768
RWKV_HEAD = 64
RWKV_PAIRS = RWKV_W // 128
RET_HEADS = RET_W // 128
GDN_HEADS = GDN_W // 128
LANES = 128
RWKV_DECAY_SCALE = 0.6065306597126334
RWKV_GN_EPS = 64e-5
RET_GN_EPS = 1e-5
ROPE_BASE = 10000.0
GDN_CONV = 4
MOE_GROUPS = 4
MOE_PER_GROUP = 8
MOE_EXPERTS = 32
MOE_HIDDEN = 256
NORM_EPS = 1e-6
L2_EPS = 1e-6
N_MAIN = 4 * RWKV_W + 4 * RET_W + 4 * GDN_W
N_EXT = 7 * LANES
CHUNK = 64
RET_CHUNK = 128
VMEM_LIMIT = 56 * 1024 * 1024


def _cparams(sem):
    return pltpu.CompilerParams(dimension_semantics=sem, vmem_limit_bytes=VMEM_LIMIT)


def _dot(a, b):
    return jnp.dot(a.astype(BF16), b.astype(BF16), preferred_element_type=F32)


def _dot_nt(a, b):
    return lax.dot_general(a.astype(BF16), b.astype(BF16), (((1,), (1,)), ((), ())),
                           preferred_element_type=F32)


def _dot_tn(a, b):
    return lax.dot_general(a.astype(BF16), b.astype(BF16), (((0,), (0,)), ((), ())),
                           preferred_element_type=F32)


def _sigmoid(x):
    return 1.0 / (1.0 + jnp.exp(-x))


def _silu(x):
    return x * _sigmoid(x)


def _shift_rows(prev8, x, n):
    xs = jnp.concatenate([prev8, x], axis=0)
    return pltpu.roll(xs, n, axis=0)[8:]


def _cumsum_rows(x, tril):
    hi = x.astype(BF16)
    lo = (x - hi.astype(F32)).astype(BF16)
    return (jnp.dot(tril, hi, preferred_element_type=F32)
            + jnp.dot(tril, lo, preferred_element_type=F32))


def _unit_lower_inverses(lows, eye, expand, steps):
    xs = [eye + low for low in lows]
    ps = list(lows)
    for _ in range(steps):
        ps = [_dot(p, expand(p)) for p in ps]
        xs = [x + _dot(x, expand(p)) for x, p in zip(xs, ps)]
    return xs


def _rmsnorm_kernel(x_ref, w_ref, o_ref):
    x = x_ref[...]
    y = x * lax.rsqrt(jnp.mean(x * x, axis=-1, keepdims=True) + NORM_EPS) * w_ref[...]
    o_ref[...] = y.astype(o_ref.dtype)


def _rmsnorm(x, w, out_dtype, tm=512):
    t, d = x.shape
    return pl.pallas_call(
        _rmsnorm_kernel,
        grid=(t // tm,),
        in_specs=[pl.BlockSpec((tm, d), lambda i: (i, 0)), pl.BlockSpec((1, d), lambda i: (0, 0))],
        out_specs=pl.BlockSpec((tm, d), lambda i: (i, 0)),
        out_shape=jax.ShapeDtypeStruct((t, d), out_dtype),
        compiler_params=_cparams(("parallel",)),
        name="rmsnorm",
    )(x, w.reshape(1, d))


def _matmul_kernel(a_ref, b_ref, o_ref):
    o_ref[...] = jnp.dot(a_ref[...], b_ref[...], preferred_element_type=F32).astype(o_ref.dtype)


def _matmul(a, b, tm, tn, out_dtype=F32):
    m, k = a.shape
    n = b.shape[1]
    return pl.pallas_call(
        _matmul_kernel,
        grid=(n // tn, m // tm),
        in_specs=[pl.BlockSpec((tm, k), lambda j, i: (i, 0)), pl.BlockSpec((k, tn), lambda j, i: (0, j))],
        out_specs=pl.BlockSpec((tm, tn), lambda j, i: (i, j)),
        out_shape=jax.ShapeDtypeStruct((m, n), out_dtype),
        compiler_params=_cparams(("parallel", "parallel")),
        name="in_proj",
    )(a, b)


def _outproj_kernel(a_ref, b_ref, c_ref, wa_ref, wb_ref, wc_ref, x_ref, o_ref):
    acc = x_ref[...]
    acc = acc + jnp.dot(a_ref[...], wa_ref[...], preferred_element_type=F32)
    acc = acc + jnp.dot(b_ref[...], wb_ref[...], preferred_element_type=F32)
    acc = acc + jnp.dot(c_ref[...], wc_ref[...], preferred_element_type=F32)
    o_ref[...] = acc


def _outproj(oa, ob, oc, w_out, x, tm=512):
    t, d = x.shape
    wa = w_out[:RWKV_W]
    wb = w_out[RWKV_W:RWKV_W + RET_W]
    wc = w_out[RWKV_W + RET_W:]
    row = lambda i: (i, 0)
    const = lambda i: (0, 0)
    return pl.pallas_call(
        _outproj_kernel,
        grid=(t // tm,),
        in_specs=[pl.BlockSpec((tm, RWKV_W), row), pl.BlockSpec((tm, RET_W), row),
                  pl.BlockSpec((tm, GDN_W), row),
                  pl.BlockSpec((RWKV_W, d), const), pl.BlockSpec((RET_W, d), const),
                  pl.BlockSpec((GDN_W, d), const), pl.BlockSpec((tm, d), row)],
        out_specs=pl.BlockSpec((tm, d), row),
        out_shape=jax.ShapeDtypeStruct((t, d), F32),
        compiler_params=_cparams(("parallel",)),
        name="out_proj",
    )(oa, ob, oc, wa, wb, wc, x)


def _rope_kernel(pos_ref, inv_ref, cos_ref, sin_ref):
    ang = pos_ref[...].astype(F32) * inv_ref[...]
    lane = lax.broadcasted_iota(jnp.int32, ang.shape, 1)
    s = jnp.sin(ang)
    cos_ref[...] = jnp.cos(ang)
    sin_ref[...] = jnp.where(lane < LANES // 2, -s, s)


def _rope_tables(positions, tm=512):
    t = positions.size
    inv = ROPE_BASE ** (-jnp.arange(0, LANES, 2, dtype=F32) / LANES)
    inv = jnp.concatenate([inv, inv]).reshape(1, LANES)
    tm = min(tm, t)
    return pl.pallas_call(
        _rope_kernel,
        grid=(t // tm,),
        in_specs=[pl.BlockSpec((tm, 1), lambda i: (i, 0)), pl.BlockSpec((1, LANES), lambda i: (0, 0))],
        out_specs=[pl.BlockSpec((tm, LANES), lambda i: (i, 0))] * 2,
        out_shape=[jax.ShapeDtypeStruct((t, LANES), F32)] * 2,
        compiler_params=_cparams(("parallel",)),
        name="rope_tables",
    )(positions.reshape(t, 1), inv)


def _rwkv_kernel(*refs, tt, nb, has_vres):
    if has_vres:
        (r_ref, k_ref, v_ref, g_ref, ext_ref, mu_ref, par_ref, w2_ref, a2_ref, v2_ref, vf_ref,
         out_ref, state_ref, carry_ref, ecarry_ref) = refs
    else:
        (r_ref, k_ref, v_ref, g_ref, ext_ref, mu_ref, par_ref, w2_ref, a2_ref,
         out_ref, vf_out_ref, state_ref, carry_ref, ecarry_ref) = refs
    c = CHUNK
    hd = RWKV_HEAD
    nc = tt // c

    @pl.when(pl.program_id(1) == 0)
    def _():
        state_ref[...] = jnp.zeros_like(state_ref)
        carry_ref[...] = jnp.zeros_like(carry_ref)
        ecarry_ref[...] = jnp.zeros_like(ecarry_ref)

    lane = lax.broadcasted_iota(jnp.int32, (1, LANES), 1)
    first = lane < hd

    def head_sum(x):
        s1 = jnp.sum(jnp.where(first, x, 0.0), axis=-1, keepdims=True)
        s2 = jnp.sum(jnp.where(first, 0.0, x), axis=-1, keepdims=True)
        return jnp.where(first, s1, s2)

    def expand(x):
        return jnp.concatenate([jnp.where(first, x, 0.0), jnp.where(first, 0.0, x)], axis=0)

    mu = mu_ref[...]
    par = par_ref[...]
    w0, a0, v0, k_k, k_a, ln_w, ln_b, r_k = (par[i:i + 1] for i in range(8))

    row = lax.broadcasted_iota(jnp.int32, (c, 2 * c), 0)
    col = lax.broadcasted_iota(jnp.int32, (c, 2 * c), 1) & (c - 1)
    strict = col < row
    incl = col <= row
    eye = jnp.where(col == row, 1.0, 0.0)
    tri_r = lax.broadcasted_iota(jnp.int32, (c, c), 0)
    tri_c = lax.broadcasted_iota(jnp.int32, (c, c), 1)
    tril = jnp.where(tri_c <= tri_r, 1.0, 0.0).astype(BF16)
    srow = lax.broadcasted_iota(jnp.int32, (LANES, LANES), 0)
    scol = lax.broadcasted_iota(jnp.int32, (LANES, LANES), 1)
    same_head = (srow < hd) == (scol < hd)

    rows = []
    for b in range(nb):
        def mixed(ref, i):
            x = ref[b]
            xs = _shift_rows(carry_ref[b, i], x, 1)
            carry_ref[b, i] = x[tt - 8:]
            return x + (xs - x) * mu[i:i + 1]

        r = mixed(r_ref, 0)
        k = mixed(k_ref, 1)
        v = mixed(v_ref, 2)
        g = mixed(g_ref, 3)
        ext = ext_ref[b]
        exts = _shift_rows(ecarry_ref[b], ext, 1)
        ecarry_ref[b] = ext[tt - 8:]

        def low_rank(i):
            return (ext[:, 2 * i * LANES:(2 * i + 1) * LANES]
                    + exts[:, (2 * i + 1) * LANES:(2 * i + 2) * LANES])

        log_w = -RWKV_DECAY_SCALE * _sigmoid(w0 + _dot(jnp.tanh(low_rank(0)), w2_ref[...]))
        a = _sigmoid(a0 + _dot(low_rank(1), a2_ref[...]))
        if has_vres:
            v = v + (vf_ref[b] - v) * _sigmoid(v0 + _dot(low_rank(2), v2_ref[...]))
        else:
            vf_out_ref[b] = v
        kk = k * k_k
        kk = kk * lax.rsqrt(head_sum(kk * kk) + L2_EPS)
        k = k * (1.0 + (a - 1.0) * k_a)
        bonus = head_sum(r * k * r_k) * v
        rows.append(dict(r=r, k=k, v=v, g=g, log_w=log_w, kk=kk, kka=kk * a, bonus=bonus))

    items = [(ci, b) for ci in range(nc) for b in range(nb)]
    pre = {}
    for ci, b in items:
        sl = slice(ci * c, (ci + 1) * c)
        rw = rows[b]
        lw = rw["log_w"][sl]
        gc = _cumsum_rows(lw, tril)
        g_last = gc[c - 1:c]
        dec_inv = jnp.exp(-gc)
        dec_out = jnp.exp(g_last - gc)
        a_t = -rw["kk"][sl] * jnp.exp(gc - lw)
        r_t = rw["r"][sl] * jnp.exp(gc)
        b_t = rw["kka"][sl] * dec_inv
        k_t = rw["k"][sl] * dec_inv
        sc = _dot_nt(jnp.concatenate([a_t, r_t], axis=0),
                     jnp.concatenate([expand(b_t), expand(k_t)], axis=0))
        pre[ci, b] = dict(
            a_t=a_t, r_t=r_t, v=rw["v"][sl],
            l_ab=jnp.where(strict, sc[:c, :2 * c], 0.0),
            l_ak=jnp.where(strict, sc[:c, 2 * c:], 0.0),
            m_rb=jnp.where(incl, sc[c:, :2 * c], 0.0),
            m_rk=jnp.where(incl, sc[c:, 2 * c:], 0.0),
            kd=jnp.concatenate([rw["kka"][sl] * dec_out, rw["k"][sl] * dec_out], axis=0),
            p_end=jnp.exp(g_last))
    t_invs = _unit_lower_inverses([pre[it]["l_ab"] for it in items], eye, expand, 5)
    for it, t_inv in zip(items, t_invs):
        pre[it]["t_inv"] = t_inv
        pre[it]["lakv"] = _dot(pre[it]["l_ak"], expand(pre[it]["v"]))

    states = [state_ref[b] for b in range(nb)]
    for ci in range(nc):
        sl = slice(ci * c, (ci + 1) * c)
        for b in range(nb):
            p = pre[ci, b]
            rw = rows[b]
            state = states[b]
            rhs = _dot_nt(p["a_t"], state) + p["lakv"]
            u = _dot(p["t_inv"], expand(rhs))
            o = _dot_nt(p["r_t"], state) + _dot(p["m_rb"], expand(u)) + _dot(p["m_rk"], expand(p["v"]))
            upd = _dot_tn(jnp.concatenate([u, p["v"]], axis=0), p["kd"])
            states[b] = state * p["p_end"] + jnp.where(same_head, upd, 0.0)

            mean = head_sum(o) * (1.0 / hd)
            cen = o - mean
            var = head_sum(cen * cen) * (1.0 / hd)
            y = cen * lax.rsqrt(var + RWKV_GN_EPS) * ln_w + ln_b
            out_ref[b, sl, :] = ((y + rw["bonus"][sl]) * _sigmoid(rw["g"][sl])).astype(out_ref.dtype)
    for b in range(nb):
        state_ref[b] = states[b]


def _rwkv_mix(pm, pe, mu_rkvg, par, w2, a2, v2, v_first, tt):
    nb, seq, _ = pm.shape
    nt = seq // tt
    has_vres = v_first is not None
    npair = RWKV_PAIRS

    def col(off):
        return pl.BlockSpec((nb, tt, LANES), lambda p, i: (0, i, off + p))

    pcol = lambda nrows: pl.BlockSpec((nrows, LANES), lambda p, i: (0, p))
    in_specs = [col(0), col(npair), col(2 * npair), col(3 * npair),
                pl.BlockSpec((nb, tt, 6 * LANES), lambda p, i: (0, i, 0)),
                pcol(4), pcol(8), pcol(LANES), pcol(LANES)]
    args = [pm, pm, pm, pm, pe, mu_rkvg, par, w2, a2]
    out_block = pl.BlockSpec((nb, tt, LANES), lambda p, i: (0, i, p))
    if has_vres:
        in_specs += [pcol(LANES), out_block]
        args += [v2, v_first]
        out_specs = out_block
        out_shape = jax.ShapeDtypeStruct((nb, seq, RWKV_W), BF16)
    else:
        out_specs = [out_block, out_block]
        out_shape = [jax.ShapeDtypeStruct((nb, seq, RWKV_W), BF16),
                     jax.ShapeDtypeStruct((nb, seq, RWKV_W), F32)]
    res = pl.pallas_call(
        functools.partial(_rwkv_kernel, tt=tt, nb=nb, has_vres=has_vres),
        grid=(npair, nt),
        in_specs=in_specs,
        out_specs=out_specs,
        out_shape=out_shape,
        scratch_shapes=[pltpu.VMEM((nb, LANES, LANES), F32), pltpu.VMEM((nb, 4, 8, LANES), F32),
                        pltpu.VMEM((nb, 8, 6 * LANES), F32)],
        compiler_params=_cparams(("parallel", "arbitrary")),
        name="rwkv7_vres" if has_vres else "rwkv7",
    )(*args)
    if has_vres:
        return res, v_first
    return res[0], res[1]


def _ret_kernel(q_ref, k_ref, v_ref, g_ref, cos_ref, sin_ref, gn_ref, out_ref, state_ref, *, tt, nb):
    c = RET_CHUNK

    @pl.when(pl.program_id(1) == 0)
    def _():
        state_ref[...] = jnp.zeros_like(state_ref)

    head = pl.program_id(0).astype(F32)
    log_gamma = jnp.log(1.0 - jnp.exp2(jnp.full((1, 1), -5.0, F32) - head))
    ri = lax.broadcasted_iota(jnp.int32, (c, c), 0)
    ci_ = lax.broadcasted_iota(jnp.int32, (c, c), 1)
    diff = (ri - ci_).astype(F32)
    dmask = jnp.where(diff >= 0, jnp.exp(jnp.maximum(diff, 0.0) * log_gamma), 0.0)
    idx = lax.broadcasted_iota(jnp.int32, (c, 1), 0).astype(F32)
    k_dec = jnp.exp((c - 1.0 - idx) * log_gamma)
    q_dec = jnp.exp((idx + 1.0) * log_gamma)
    chunk_decay = jnp.exp(c * log_gamma)

    qs, ks = [], []
    for b in range(nb):
        cos = cos_ref[b]
        sin = sin_ref[b]

        def rotary(x):
            return x * cos + pltpu.roll(x, LANES // 2, axis=1) * sin

        qs.append(rotary(q_ref[b]))
        ks.append(rotary(k_ref[b]) * (LANES ** -0.5))

    states = [state_ref[b] for b in range(nb)]
    for i in range(tt // c):
        sl = slice(i * c, (i + 1) * c)
        for b in range(nb):
            qc, kc, vc = qs[b][sl], ks[b][sl], v_ref[b, sl, :]
            scores = _dot_nt(qc, kc) * dmask
            y = _dot(scores, vc) + _dot(qc * q_dec, states[b])
            states[b] = states[b] * chunk_decay + _dot_tn(kc * k_dec, vc)
            mean = jnp.mean(y, axis=-1, keepdims=True)
            cen = y - mean
            var = jnp.mean(cen * cen, axis=-1, keepdims=True)
            yn = cen * lax.rsqrt(var + RET_GN_EPS) * gn_ref[...]
            out_ref[b, sl, :] = (_silu(g_ref[b, sl, :]) * yn).astype(out_ref.dtype)
    for b in range(nb):
        state_ref[b] = states[b]


def _ret_mix(pm, cos_t, sin_t, gn_w, tt):
    nb, seq, _ = pm.shape
    nt = seq // tt
    base = 4 * RWKV_PAIRS

    def col(off):
        return pl.BlockSpec((nb, tt, LANES), lambda h, i: (0, i, base + off + h))

    tab = pl.BlockSpec((nb, tt, LANES), lambda h, i: (0, i, 0))
    return pl.pallas_call(
        functools.partial(_ret_kernel, tt=tt, nb=nb),
        grid=(RET_HEADS, nt),
        in_specs=[col(0), col(RET_HEADS), col(2 * RET_HEADS), col(3 * RET_HEADS), tab, tab,
                  pl.BlockSpec((1, LANES), lambda h, i: (0, h))],
        out_specs=pl.BlockSpec((nb, tt, LANES), lambda h, i: (0, i, h)),
        out_shape=jax.ShapeDtypeStruct((nb, seq, RET_W), BF16),
        scratch_shapes=[pltpu.VMEM((nb, LANES, LANES), F32)],
        compiler_params=_cparams(("parallel", "arbitrary")),
        name="retention",
    )(pm, pm, pm, pm, cos_t, sin_t, gn_w.reshape(1, RET_W))


def _gdn_kernel(q_ref, k_ref, v_ref, z_ref, ab_ref, cw_ref, par_ref, nw_ref,
                out_ref, state_ref, carry_ref, *, tt, nb):
    c = CHUNK
    nc = tt // c
    head = pl.program_id(0)

    @pl.when(pl.program_id(1) == 0)
    def _():
        state_ref[...] = jnp.zeros_like(state_ref)
        carry_ref[...] = jnp.zeros_like(carry_ref)

    cw = cw_ref[...]
    par = par_ref[...]
    neg_rate = -jnp.exp(par[:, 0:1])
    dt_bias = par[:, 1:2]

    def softplus(x):
        return jnp.maximum(x, 0.0) + jnp.log(1.0 + jnp.exp(-jnp.abs(x)))

    def l2norm(x):
        return x * lax.rsqrt(jnp.sum(x * x, axis=-1, keepdims=True) + L2_EPS)

    ri = lax.broadcasted_iota(jnp.int32, (c, c), 0)
    ci_ = lax.broadcasted_iota(jnp.int32, (c, c), 1)
    causal = ci_ <= ri
    strict = ci_ < ri
    eye = jnp.where(ci_ == ri, 1.0, 0.0)
    lane = lax.broadcasted_iota(jnp.int32, (1, LANES), 1)
    sub = lax.broadcasted_iota(jnp.int32, (2 * 8, 1), 0)

    rows = []
    for b in range(nb):
        def conv_silu(ref, i):
            x = ref[b]
            prev = carry_ref[b, i]
            acc = x * cw[i, GDN_CONV - 1:GDN_CONV]
            for j in range(GDN_CONV - 1):
                acc = acc + _shift_rows(prev, x, GDN_CONV - 1 - j) * cw[i, j:j + 1]
            carry_ref[b, i] = x[tt - 8:]
            return _silu(acc)

        q = l2norm(conv_silu(q_ref, 0)) * (LANES ** -0.5)
        k = l2norm(conv_silu(k_ref, 1))
        v = conv_silu(v_ref, 2)
        ab = ab_ref[b]
        a_col = jnp.sum(jnp.where(lane == head, ab, 0.0), axis=-1, keepdims=True)
        b_col = jnp.sum(jnp.where(lane == head + GDN_HEADS, ab, 0.0), axis=-1, keepdims=True)
        ab_t = ab.T[:2 * 8]
        a_row = jnp.sum(jnp.where(sub == head, ab_t, 0.0), axis=0, keepdims=True)
        rows.append(dict(q=q, k=k, v=v,
                         g_col=neg_rate * softplus(a_col + dt_bias),
                         g_row=neg_rate * softplus(a_row + dt_bias),
                         beta=_sigmoid(b_col)))

    items = [(ci, b) for ci in range(nc) for b in range(nb)]
    pre = {}
    for ci, b in items:
        sl = slice(ci * c, (ci + 1) * c)
        rw = rows[b]
        gr = rw["g_row"][:, sl]
        gcl = rw["g_col"][sl]
        gc_col = jnp.sum(jnp.where(causal, gr, 0.0), axis=1, keepdims=True)
        gc_row = jnp.sum(jnp.where(ri <= ci_, gcl, 0.0), axis=0, keepdims=True)
        g_last = gc_col[c - 1:c]
        decay = jnp.where(causal, jnp.exp(jnp.minimum(gc_col - gc_row, 0.0)), 0.0)
        qc, kc, vc, bc = rw["q"][sl], rw["k"][sl], rw["v"][sl], rw["beta"][sl]
        kb = kc * bc
        kk_t = _dot_nt(jnp.concatenate([kb, qc], axis=0), kc)
        e_gc = jnp.exp(gc_col)
        pre[ci, b] = dict(
            lower=jnp.where(strict, kk_t[:c] * decay, 0.0),
            attn=kk_t[c:] * decay,
            vk=jnp.concatenate([vc * bc, kb * e_gc], axis=1),
            q_in=qc * e_gc,
            k_out=kc * jnp.exp(g_last - gc_col),
            p_end=jnp.exp(g_last))
    t_invs = _unit_lower_inverses([-pre[it]["lower"] for it in items], eye, lambda x: x, 5)
    for it, t_inv in zip(items, t_invs):
        pre[it]["uw"] = _dot(t_inv, pre[it]["vk"])

    states = [state_ref[b] for b in range(nb)]
    for ci in range(nc):
        sl = slice(ci * c, (ci + 1) * c)
        for b in range(nb):
            p = pre[ci, b]
            state = states[b]
            v_new = p["uw"][:, :LANES] - _dot(p["uw"][:, LANES:], state)
            o = _dot(p["q_in"], state) + _dot(p["attn"], v_new)
            states[b] = state * p["p_end"] + _dot_tn(p["k_out"], v_new)
            on = o * lax.rsqrt(jnp.mean(o * o, axis=-1, keepdims=True) + NORM_EPS)
            out_ref[b, sl, :] = (on * nw_ref[...] * _silu(z_ref[b, sl, :])).astype(out_ref.dtype)
    for b in range(nb):
        state_ref[b] = states[b]


def _gdn_mix(pm, pe, conv_w, par, norm_w, tt):
    nb, seq, _ = pm.shape
    nt = seq // tt
    base = 4 * RWKV_PAIRS + 4 * RET_HEADS

    def col(off):
        return pl.BlockSpec((nb, tt, LANES), lambda h, i: (0, i, base + off + h))

    return pl.pallas_call(
        functools.partial(_gdn_kernel, tt=tt, nb=nb),
        grid=(GDN_HEADS, nt),
        in_specs=[col(0), col(GDN_HEADS), col(2 * GDN_HEADS), col(3 * GDN_HEADS),
                  pl.BlockSpec((nb, tt, LANES), lambda h, i: (0, i, 6)),
                  pl.BlockSpec((3, GDN_CONV, LANES), lambda h, i: (0, 0, h)),
                  pl.BlockSpec((None, 1, LANES), lambda h, i: (h, 0, 0)),
                  pl.BlockSpec((1, LANES), lambda h, i: (0, h))],
        out_specs=pl.BlockSpec((nb, tt, LANES), lambda h, i: (0, i, h)),
        out_shape=jax.ShapeDtypeStruct((nb, seq, GDN_W), BF16),
        scratch_shapes=[pltpu.VMEM((nb, LANES, LANES), F32), pltpu.VMEM((nb, 3, 8, LANES), F32)],
        compiler_params=_cparams(("parallel", "arbitrary")),
        name="gated_deltanet",
    )(pm, pm, pm, pm, pe, conv_w, par, norm_w.reshape(1, GDN_W))


def _router_kernel(x_ref, nw_ref, wr_ref, br_ref, h_ref, gates_ref):
    x = x_ref[...]
    h = x * lax.rsqrt(jnp.mean(x * x, axis=-1, keepdims=True) + NORM_EPS) * nw_ref[...]
    h_ref[...] = h.astype(h_ref.dtype)
    logits = jnp.dot(h, wr_ref[...], precision=lax.Precision.HIGHEST,
                     preferred_element_type=F32) + br_ref[...]
    lane_i = lax.broadcasted_iota(jnp.int32, logits.shape, 1)
    lane = lane_i.astype(F32)
    group_of_lane = (lane_i // MOE_PER_GROUP).astype(F32)
    neg = -jnp.inf
    is_group = (lane_i >= MOE_EXPERTS) & (lane_i < MOE_EXPERTS + MOE_GROUPS)
    gl = jnp.where(is_group, logits, neg)
    gmax = jnp.max(gl, axis=-1, keepdims=True)
    gidx = jnp.min(jnp.where(gl == gmax, lane - MOE_EXPERTS, LANES), axis=-1, keepdims=True)
    group_gate = 1.0 / jnp.sum(jnp.exp(gl - gmax), axis=-1, keepdims=True)
    in_group = (lane_i < MOE_EXPERTS) & (group_of_lane == gidx)
    el = jnp.where(in_group, logits, neg)
    m1 = jnp.max(el, axis=-1, keepdims=True)
    i1 = jnp.min(jnp.where(el == m1, lane, LANES), axis=-1, keepdims=True)
    el2 = jnp.where(lane == i1, neg, el)
    m2 = jnp.max(el2, axis=-1, keepdims=True)
    i2 = jnp.min(jnp.where(el2 == m2, lane, LANES), axis=-1, keepdims=True)
    esum = jnp.sum(jnp.exp(el - m1), axis=-1, keepdims=True)
    p1 = 1.0 / esum
    p2 = jnp.exp(m2 - m1) / esum
    tot = p1 + p2
    within = jnp.where(lane == i1, p1 / tot, jnp.where(lane == i2, p2 / tot, 0.0))
    gates_ref[...] = group_gate * within


def _router(x, norm_w, wr, br, tm=512):
    t, d = x.shape
    row = lambda i: (i, 0)
    const = lambda i: (0, 0)
    return pl.pallas_call(
        _router_kernel,
        grid=(t // tm,),
        in_specs=[pl.BlockSpec((tm, d), row), pl.BlockSpec((1, d), const),
                  pl.BlockSpec((d, LANES), const), pl.BlockSpec((1, LANES), const)],
        out_specs=[pl.BlockSpec((tm, d), row), pl.BlockSpec((tm, LANES), row)],
        out_shape=[jax.ShapeDtypeStruct((t, d), BF16), jax.ShapeDtypeStruct((t, LANES), F32)],
        compiler_params=_cparams(("parallel",)),
        name="moe_router",
    )(x, norm_w.reshape(1, d), wr, br)


def _moe_dense_kernel(h_ref, g_ref, w1_ref, w3_ref, w2_ref, x_ref, o_ref, acc_ref, *, te):
    j = pl.program_id(1)

    @pl.when(j == 0)
    def _():
        acc_ref[...] = x_ref[...]

    h = h_ref[...]
    gts = g_ref[...]
    lane = lax.broadcasted_iota(jnp.int32, (1, LANES), 1)
    acc = acc_ref[...]
    for e in range(te):
        gate = jnp.sum(jnp.where(lane == j * te + e, gts, 0.0), axis=-1, keepdims=True)
        a1 = jnp.dot(h, w1_ref[e], preferred_element_type=F32)
        a3 = jnp.dot(h, w3_ref[e], preferred_element_type=F32)
        hid = _silu(a1) * a3 * gate
        acc = acc + jnp.dot(hid.astype(BF16), w2_ref[e], preferred_element_type=F32)
    acc_ref[...] = acc

    @pl.when(j == pl.num_programs(1) - 1)
    def _():
        o_ref[...] = acc


def _moe_dense(h, gates, w1, w3, w2, x, tm=512, te=2):
    t, d = x.shape
    ne = MOE_EXPERTS // te
    return pl.pallas_call(
        functools.partial(_moe_dense_kernel, te=te),
        grid=(t // tm, ne),
        in_specs=[pl.BlockSpec((tm, d), lambda i, j: (i, 0)),
                  pl.BlockSpec((tm, LANES), lambda i, j: (i, 0)),
                  pl.BlockSpec((te, d, MOE_HIDDEN), lambda i, j: (j, 0, 0)),
                  pl.BlockSpec((te, d, MOE_HIDDEN), lambda i, j: (j, 0, 0)),
                  pl.BlockSpec((te, MOE_HIDDEN, d), lambda i, j: (j, 0, 0)),
                  pl.BlockSpec((tm, d), lambda i, j: (i, 0))],
        out_specs=pl.BlockSpec((tm, d), lambda i, j: (i, 0)),
        out_shape=jax.ShapeDtypeStruct((t, d), F32),
        scratch_shapes=[pltpu.VMEM((tm, d), F32)],
        compiler_params=_cparams(("parallel", "arbitrary")),
        name="moe_experts",
    )(h, gates, w1, w3, w2, x)


def _pad_cols(w, n):
    return jnp.pad(w, ((0, 0), (0, n - w.shape[1])))


def _pad_rows(w, n):
    return jnp.pad(w, ((0, n - w.shape[0]), (0, 0)))


def _mixers(l, h, batch, tt, cos_t, sin_t, v_first, w_in, rwkv_mu_rkvg, rwkv_mu_wa, rwkv_w0, rwkv_w1,
            rwkv_w2, rwkv_a0, rwkv_a1, rwkv_a2, rwkv_k_k, rwkv_k_a, rwkv_r_k, rwkv_ln_w, rwkv_ln_b,
            rwkv_mu_vres, rwkv_v0, rwkv_v1, rwkv_v2, ret_gn_w, gdn_conv_w, gdn_a_log, gdn_dt_bias,
            gdn_norm_w):
    t = h.shape[0]
    seq = t // batch
    d = D_MODEL
    w_main = w_in[l][:, :N_MAIN].astype(BF16)
    mu_w = rwkv_mu_wa[l][0][:, None]
    mu_a = rwkv_mu_wa[l][1][:, None]
    blocks = [(1.0 - mu_w) * rwkv_w1[l], mu_w * rwkv_w1[l], (1.0 - mu_a) * rwkv_a1[l], mu_a * rwkv_a1[l]]
    if l > 0:
        mu_v = rwkv_mu_vres[l - 1][:, None]
        blocks += [(1.0 - mu_v) * rwkv_v1[l - 1], mu_v * rwkv_v1[l - 1]]
        v0 = rwkv_v0[l - 1]
        v2 = _pad_rows(rwkv_v2[l - 1], LANES).astype(BF16)
    else:
        blocks += [jnp.zeros((d, LANES), F32)] * 2
        v0 = jnp.zeros((RWKV_W,), F32)
        v2 = None
    blocks.append(w_in[l][:, N_MAIN:])
    w_ext = jnp.concatenate([_pad_cols(b, LANES) for b in blocks], axis=1).astype(BF16)

    pm = _matmul(h, w_main, tm=min(1024, t), tn=1024).reshape(batch, seq, N_MAIN)
    pe = _matmul(h, w_ext, tm=min(1024, t), tn=N_EXT).reshape(batch, seq, N_EXT)

    par = jnp.stack([rwkv_w0[l], rwkv_a0[l], v0, rwkv_k_k[l], rwkv_k_a[l], rwkv_ln_w[l], rwkv_ln_b[l],
                     rwkv_r_k[l].reshape(RWKV_W)])
    out_a, v_first = _rwkv_mix(pm, pe, rwkv_mu_rkvg[l], par,
                               _pad_rows(rwkv_w2[l], LANES).astype(BF16),
                               _pad_rows(rwkv_a2[l], LANES).astype(BF16), v2, v_first, tt)
    out_b = _ret_mix(pm, cos_t, sin_t, ret_gn_w[l], tt)
    gpar = _pad_cols(jnp.stack([gdn_a_log[l], gdn_dt_bias[l]], axis=1), LANES).reshape(GDN_HEADS, 1, LANES)
    out_c = _gdn_mix(pm, pe, gdn_conv_w[l].reshape(GDN_CONV, 3, GDN_W).transpose(1, 0, 2), gpar,
                     gdn_norm_w[l], tt)
    return (out_a.reshape(t, RWKV_W), out_b.reshape(t, RET_W), out_c.reshape(t, GDN_W), v_first)


def kernel(x, positions, norm1_w, w_in, w_out, rwkv_mu_rkvg, rwkv_mu_wa, rwkv_w0, rwkv_w1, rwkv_w2, rwkv_a0, rwkv_a1, rwkv_a2, rwkv_k_k, rwkv_k_a, rwkv_r_k, rwkv_ln_w, rwkv_ln_b, rwkv_mu_vres, rwkv_v0, rwkv_v1, rwkv_v2, ret_gn_w, gdn_conv_w, gdn_a_log, gdn_dt_bias, gdn_norm_w, norm2_w, moe_group_w, moe_group_b, moe_expert_w, moe_expert_b, moe_w1, moe_w3, moe_w2, final_norm_w):
    batch, seq, d = x.shape
    t = batch * seq
    depth = w_in.shape[0]
    tt = min(256, seq)
    tm = min(512, t)
    xf = x.reshape(t, d)
    cos_t, sin_t = _rope_tables(positions, tm)
    cos_t = cos_t.reshape(batch, seq, LANES)
    sin_t = sin_t.reshape(batch, seq, LANES)
    v_first = None
    for l in range(depth):
        h = _rmsnorm(xf, norm1_w[l], BF16, tm)
        out_a, out_b, out_c, v_first = _mixers(
            l, h, batch, tt, cos_t, sin_t, v_first, w_in, rwkv_mu_rkvg, rwkv_mu_wa, rwkv_w0, rwkv_w1,
            rwkv_w2, rwkv_a0, rwkv_a1, rwkv_a2, rwkv_k_k, rwkv_k_a, rwkv_r_k, rwkv_ln_w, rwkv_ln_b,
            rwkv_mu_vres, rwkv_v0, rwkv_v1, rwkv_v2, ret_gn_w, gdn_conv_w, gdn_a_log, gdn_dt_bias,
            gdn_norm_w)
        xf = _outproj(out_a, out_b, out_c, w_out[l].astype(BF16), xf, tm)
        wr = _pad_cols(jnp.concatenate([moe_expert_w[l], moe_group_w[l]], axis=1), LANES)
        br = _pad_cols(jnp.concatenate([moe_expert_b[l], moe_group_b[l]]).reshape(1, -1), LANES)
        h2, gates = _router(xf, norm2_w[l], wr, br, tm)
        xf = _moe_dense(h2, gates, moe_w1[l].astype(BF16), moe_w3[l].astype(BF16),
                        moe_w2[l].astype(BF16), xf, tm)
    out = _rmsnorm(xf, final_norm_w, F32, tm)
    return out.reshape(batch, seq, d)
```

```python
import functools

import jax
import jax.numpy as jnp
from jax import lax
from jax.experimental import pallas as pl
from jax.experimental.pallas import tpu as pltpu

F32 = jnp.float32
BF16 = jnp.bfloat16

D_MODEL = 2048
RWKV_W = 768
RET_W = 512
GDN_W = 768
RWKV_HEAD = 64
RWKV_PAIRS = RWKV_W // 128
RET_HEADS = RET_W // 128
GDN_HEADS = GDN_W // 128
LANES = 128
RWKV_DECAY_SCALE = 0.6065306597126334
RWKV_GN_EPS = 64e-5
RET_GN_EPS = 1e-5
ROPE_BASE = 10000.0
GDN_CONV = 4
MOE_GROUPS = 4
MOE_PER_GROUP = 8
MOE_EXPERTS = 32
MOE_HIDDEN = 256
NORM_EPS = 1e-6
L2_EPS = 1e-6
N_MAIN = 4 * RWKV_W + 4 * RET_W + 4 * GDN_W
N_EXT = 7 * LANES
CHUNK = 64
RET_CHUNK = 128
VMEM_LIMIT = 56 * 1024 * 1024


def _cparams(sem):
    return pltpu.CompilerParams(dimension_semantics=sem, vmem_limit_bytes=VMEM_LIMIT)


def _dot(a, b):
    return jnp.dot(a.astype(BF16), b.astype(BF16), preferred_element_type=F32)


def _dot_nt(a, b):
    return lax.dot_general(a.astype(BF16), b.astype(BF16), (((1,), (1,)), ((), ())),
                           preferred_element_type=F32)


def _dot_tn(a, b):
    return lax.dot_general(a.astype(BF16), b.astype(BF16), (((0,), (0,)), ((), ())),
                           preferred_element_type=F32)


def _sigmoid(x):
    return 1.0 / (1.0 + jnp.exp(-x))


def _silu(x):
    return x * _sigmoid(x)


def _shift_rows(prev8, x, n):
    xs = jnp.concatenate([prev8, x], axis=0)
    return pltpu.roll(xs, n, axis=0)[8:]


def _cumsum_rows(x, tril):
    hi = x.astype(BF16)
    lo = (x - hi.astype(F32)).astype(BF16)
    return (jnp.dot(tril, hi, preferred_element_type=F32)
            + jnp.dot(tril, lo, preferred_element_type=F32))


def _unit_lower_inverses(lows, eye, expand, steps):
    xs = [eye + low for low in lows]
    ps = list(lows)
    for _ in range(steps):
        ps = [_dot(p, expand(p)) for p in ps]
        xs = [x + _dot(x, expand(p)) for x, p in zip(xs, ps)]
    return xs


def _rmsnorm_kernel(x_ref, w_ref, o_ref):
    x = x_ref[...]
    y = x * lax.rsqrt(jnp.mean(x * x, axis=-1, keepdims=True) + NORM_EPS) * w_ref[...]
    o_ref[...] = y.astype(o_ref.dtype)


def _rmsnorm(x, w, out_dtype, tm=512):
    t, d = x.shape
    return pl.pallas_call(
        _rmsnorm_kernel,
        grid=(t // tm,),
        in_specs=[pl.BlockSpec((tm, d), lambda i: (i, 0)), pl.BlockSpec((1, d), lambda i: (0, 0))],
        out_specs=pl.BlockSpec((tm, d), lambda i: (i, 0)),
        out_shape=jax.ShapeDtypeStruct((t, d), out_dtype),
        compiler_params=_cparams(("parallel",)),
        name="rmsnorm",
    )(x, w.reshape(1, d))


def _matmul_nt_kernel(a_ref, bt_ref, o_ref):
    o_ref[...] = lax.dot_general(a_ref[...], bt_ref[...], (((1,), (1,)), ((), ())),
                                 preferred_element_type=F32).astype(o_ref.dtype)


def _matmul_nt(a, bt, n, tm, tn, out_dtype=F32):
    m, k = a.shape
    return pl.pallas_call(
        _matmul_nt_kernel,
        grid=(n // tn, m // tm),
        in_specs=[pl.BlockSpec((tm, k), lambda j, i: (i, 0)), pl.BlockSpec((tn, k), lambda j, i: (j, 0))],
        out_specs=pl.BlockSpec((tm, tn), lambda j, i: (i, j)),
        out_shape=jax.ShapeDtypeStruct((m, n), out_dtype),
        compiler_params=_cparams(("parallel", "parallel")),
        name="in_proj",
    )(a, bt)


def _outproj_kernel(a_ref, b_ref, c_ref, w_ref, x_ref, o_ref):
    acc = x_ref[...]
    acc = acc + jnp.dot(a_ref[...], w_ref[:RWKV_W], preferred_element_type=F32)
    acc = acc + jnp.dot(b_ref[...], w_ref[RWKV_W:RWKV_W + RET_W], preferred_element_type=F32)
    acc = acc + jnp.dot(c_ref[...], w_ref[RWKV_W + RET_W:], preferred_element_type=F32)
    o_ref[...] = acc


def _outproj(oa, ob, oc, w_out, x, l, tm=512):
    t, d = x.shape
    row = lambda i: (i, 0)
    return pl.pallas_call(
        _outproj_kernel,
        grid=(t // tm,),
        in_specs=[pl.BlockSpec((tm, RWKV_W), row), pl.BlockSpec((tm, RET_W), row),
                  pl.BlockSpec((tm, GDN_W), row),
                  pl.BlockSpec((None, d, d), lambda i: (l, 0, 0)), pl.BlockSpec((tm, d), row)],
        out_specs=pl.BlockSpec((tm, d), row),
        out_shape=jax.ShapeDtypeStruct((t, d), F32),
        compiler_params=_cparams(("parallel",)),
        name="out_proj",
    )(oa, ob, oc, w_out, x)


def _rope_kernel(pos_ref, inv_ref, cos_ref, sin_ref):
    ang = pos_ref[...].astype(F32) * inv_ref[...]
    lane = lax.broadcasted_iota(jnp.int32, ang.shape, 1)
    s = jnp.sin(ang)
    cos_ref[...] = jnp.cos(ang)
    sin_ref[...] = jnp.where(lane < LANES // 2, -s, s)


def _rope_tables(positions, tm=512):
    t = positions.size
    inv = ROPE_BASE ** (-jnp.arange(0, LANES, 2, dtype=F32) / LANES)
    inv = jnp.concatenate([inv, inv]).reshape(1, LANES)
    tm = min(tm, t)
    return pl.pallas_call(
        _rope_kernel,
        grid=(t // tm,),
        in_specs=[pl.BlockSpec((tm, 1), lambda i: (i, 0)), pl.BlockSpec((1, LANES), lambda i: (0, 0))],
        out_specs=[pl.BlockSpec((tm, LANES), lambda i: (i, 0))] * 2,
        out_shape=[jax.ShapeDtypeStruct((t, LANES), F32)] * 2,
        compiler_params=_cparams(("parallel",)),
        name="rope_tables",
    )(positions.reshape(t, 1), inv)


def _rwkv_kernel(*refs, tt, nb, has_vres):
    if has_vres:
        (r_ref, k_ref, v_ref, g_ref, ext_ref, mu_ref, par_ref, w2_ref, a2_ref, v2_ref, vf_ref,
         out_ref, state_ref, carry_ref, ecarry_ref) = refs
    else:
        (r_ref, k_ref, v_ref, g_ref, ext_ref, mu_ref, par_ref, w2_ref, a2_ref,
         out_ref, vf_out_ref, state_ref, carry_ref, ecarry_ref) = refs
    c = CHUNK
    hd = RWKV_HEAD
    nc = tt // c

    @pl.when(pl.program_id(1) == 0)
    def _():
        state_ref[...] = jnp.zeros_like(state_ref)
        carry_ref[...] = jnp.zeros_like(carry_ref)
        ecarry_ref[...] = jnp.zeros_like(ecarry_ref)

    lane = lax.broadcasted_iota(jnp.int32, (1, LANES), 1)
    first = lane < hd

    def head_sum(x):
        s1 = jnp.sum(jnp.where(first, x, 0.0), axis=-1, keepdims=True)
        s2 = jnp.sum(jnp.where(first, 0.0, x), axis=-1, keepdims=True)
        return jnp.where(first, s1, s2)

    def expand(x):
        return jnp.concatenate([jnp.where(first, x, 0.0), jnp.where(first, 0.0, x)], axis=0)

    mu = mu_ref[...]
    par = par_ref[...]
    w0, a0, v0, k_k, k_a, ln_w, ln_b, r_k = (par[i:i + 1] for i in range(8))

    row = lax.broadcasted_iota(jnp.int32, (c, 2 * c), 0)
    col = lax.broadcasted_iota(jnp.int32, (c, 2 * c), 1) & (c - 1)
    strict = col < row
    incl = col <= row
    eye = jnp.where(col == row, 1.0, 0.0)
    tri_r = lax.broadcasted_iota(jnp.int32, (c, c), 0)
    tri_c = lax.broadcasted_iota(jnp.int32, (c, c), 1)
    tril = jnp.where(tri_c <= tri_r, 1.0, 0.0).astype(BF16)
    srow = lax.broadcasted_iota(jnp.int32, (LANES, LANES), 0)
    scol = lax.broadcasted_iota(jnp.int32, (LANES, LANES), 1)
    same_head = (srow < hd) == (scol < hd)

    rows = []
    for b in range(nb):
        def mixed(ref, i):
            x = ref[b]
            xs = _shift_rows(carry_ref[b, i], x, 1)
            carry_ref[b, i] = x[tt - 8:]
            return x + (xs - x) * mu[i:i + 1]

        r = mixed(r_ref, 0)
        k = mixed(k_ref, 1)
        v = mixed(v_ref, 2)
        g = mixed(g_ref, 3)
        ext = ext_ref[b]
        exts = _shift_rows(ecarry_ref[b], ext, 1)
        ecarry_ref[b] = ext[tt - 8:]

        def low_rank(i):
            return (ext[:, 2 * i * LANES:(2 * i + 1) * LANES]
                    + exts[:, (2 * i + 1) * LANES:(2 * i + 2) * LANES])

        log_w = -RWKV_DECAY_SCALE * _sigmoid(w0 + _dot(jnp.tanh(low_rank(0)), w2_ref[...]))
        a = _sigmoid(a0 + _dot(low_rank(1), a2_ref[...]))
        if has_vres:
            v = v + (vf_ref[b] - v) * _sigmoid(v0 + _dot(low_rank(2), v2_ref[...]))
        else:
            vf_out_ref[b] = v
        kk = k * k_k
        kk = kk * lax.rsqrt(head_sum(kk * kk) + L2_EPS)
        k = k * (1.0 + (a - 1.0) * k_a)
        bonus = head_sum(r * k * r_k) * v
        rows.append(dict(r=r, k=k, v=v, g=g, log_w=log_w, kk=kk, kka=kk * a, bonus=bonus))

    items = [(ci, b) for ci in range(nc) for b in range(nb)]
    pre = {}
    for ci, b in items:
        sl = slice(ci * c, (ci + 1) * c)
        rw = rows[b]
        lw = rw["log_w"][sl]
        gc = _cumsum_rows(lw, tril)
        g_last = gc[c - 1:c]
        dec_inv = jnp.exp(-gc)
        dec_out = jnp.exp(g_last - gc)
        a_t = -rw["kk"][sl] * jnp.exp(gc - lw)
        r_t = rw["r"][sl] * jnp.exp(gc)
        b_t = rw["kka"][sl] * dec_inv
        k_t = rw["k"][sl] * dec_inv
        sc = _dot_nt(jnp.concatenate([a_t, r_t], axis=0),
                     jnp.concatenate([expand(b_t), expand(k_t)], axis=0))
        pre[ci, b] = dict(
            a_t=a_t, r_t=r_t, v=rw["v"][sl],
            l_ab=jnp.where(strict, sc[:c, :2 * c], 0.0),
            l_ak=jnp.where(strict, sc[:c, 2 * c:], 0.0),
            m_rb=jnp.where(incl, sc[c:, :2 * c], 0.0),
            m_rk=jnp.where(incl, sc[c:, 2 * c:], 0.0),
            kd=jnp.concatenate([rw["kka"][sl] * dec_out, rw["k"][sl] * dec_out], axis=0),
            p_end=jnp.exp(g_last))
    t_invs = _unit_lower_inverses([pre[it]["l_ab"] for it in items], eye, expand, 5)
    for it, t_inv in zip(items, t_invs):
        pre[it]["t_inv"] = t_inv
        pre[it]["lakv"] = _dot(pre[it]["l_ak"], expand(pre[it]["v"]))

    states = [state_ref[b] for b in range(nb)]
    for ci in range(nc):
        sl = slice(ci * c, (ci + 1) * c)
        for b in range(nb):
            p = pre[ci, b]
            rw = rows[b]
            state = states[b]
            rhs = _dot_nt(p["a_t"], state) + p["lakv"]
            u = _dot(p["t_inv"], expand(rhs))
            o = _dot_nt(p["r_t"], state) + _dot(p["m_rb"], expand(u)) + _dot(p["m_rk"], expand(p["v"]))
            upd = _dot_tn(jnp.concatenate([u, p["v"]], axis=0), p["kd"])
            states[b] = state * p["p_end"] + jnp.where(same_head, upd, 0.0)

            mean = head_sum(o) * (1.0 / hd)
            cen = o - mean
            var = head_sum(cen * cen) * (1.0 / hd)
            y = cen * lax.rsqrt(var + RWKV_GN_EPS) * ln_w + ln_b
            out_ref[b, sl, :] = ((y + rw["bonus"][sl]) * _sigmoid(rw["g"][sl])).astype(out_ref.dtype)
    for b in range(nb):
        state_ref[b] = states[b]


def _rwkv_mix(pm, pe, mu_rkvg, par, w2, a2, v2, v_first, tt):
    nb, seq, _ = pm.shape
    nt = seq // tt
    has_vres = v_first is not None
    npair = RWKV_PAIRS

    def col(off):
        return pl.BlockSpec((nb, tt, LANES), lambda p, i: (0, i, off + p))

    pcol = lambda nrows: pl.BlockSpec((nrows, LANES), lambda p, i: (0, p))
    in_specs = [col(0), col(npair), col(2 * npair), col(3 * npair),
                pl.BlockSpec((nb, tt, 6 * LANES), lambda p, i: (0, i, 0)),
                pcol(4), pcol(8), pcol(LANES), pcol(LANES)]
    args = [pm, pm, pm, pm, pe, mu_rkvg, par, w2, a2]
    out_block = pl.BlockSpec((nb, tt, LANES), lambda p, i: (0, i, p))
    if has_vres:
        in_specs += [pcol(LANES), out_block]
        args += [v2, v_first]
        out_specs = out_block
        out_shape = jax.ShapeDtypeStruct((nb, seq, RWKV_W), BF16)
    else:
        out_specs = [out_block, out_block]
        out_shape = [jax.ShapeDtypeStruct((nb, seq, RWKV_W), BF16),
                     jax.ShapeDtypeStruct((nb, seq, RWKV_W), F32)]
    res = pl.pallas_call(
        functools.partial(_rwkv_kernel, tt=tt, nb=nb, has_vres=has_vres),
        grid=(npair, nt),
        in_specs=in_specs,
        out_specs=out_specs,
        out_shape=out_shape,
        scratch_shapes=[pltpu.VMEM((nb, LANES, LANES), F32), pltpu.VMEM((nb, 4, 8, LANES), F32),
                        pltpu.VMEM((nb, 8, 6 * LANES), F32)],
        compiler_params=_cparams(("parallel", "arbitrary")),
        name="rwkv7_vres" if has_vres else "rwkv7",
    )(*args)
    if has_vres:
        return res, v_first
    return res[0], res[1]


def _ret_kernel(q_ref, k_ref, v_ref, g_ref, cos_ref, sin_ref, gn_ref, out_ref, state_ref, *, tt, nb):
    c = RET_CHUNK

    @pl.when(pl.program_id(1) == 0)
    def _():
        state_ref[...] = jnp.zeros_like(state_ref)

    head = pl.program_id(0).astype(F32)
    log_gamma = jnp.log(1.0 - jnp.exp2(jnp.full((1, 1), -5.0, F32) - head))
    ri = lax.broadcasted_iota(jnp.int32, (c, c), 0)
    ci_ = lax.broadcasted_iota(jnp.int32, (c, c), 1)
    diff = (ri - ci_).astype(F32)
    dmask = jnp.where(diff >= 0, jnp.exp(jnp.maximum(diff, 0.0) * log_gamma), 0.0)
    idx = lax.broadcasted_iota(jnp.int32, (c, 1), 0).astype(F32)
    k_dec = jnp.exp((c - 1.0 - idx) * log_gamma)
    q_dec = jnp.exp((idx + 1.0) * log_gamma)
    chunk_decay = jnp.exp(c * log_gamma)

    qs, ks = [], []
    for b in range(nb):
        cos = cos_ref[b]
        sin = sin_ref[b]

        def rotary(x):
            return x * cos + pltpu.roll(x, LANES // 2, axis=1) * sin

        qs.append(rotary(q_ref[b]))
        ks.append(rotary(k_ref[b]) * (LANES ** -0.5))

    states = [state_ref[b] for b in range(nb)]
    for i in range(tt // c):
        sl = slice(i * c, (i + 1) * c)
        for b in range(nb):
            qc, kc, vc = qs[b][sl], ks[b][sl], v_ref[b, sl, :]
            scores = _dot_nt(qc, kc) * dmask
            y = _dot(scores, vc) + _dot(qc * q_dec, states[b])
            states[b] = states[b] * chunk_decay + _dot_tn(kc * k_dec, vc)
            mean = jnp.mean(y, axis=-1, keepdims=True)
            cen = y - mean
            var = jnp.mean(cen * cen, axis=-1, keepdims=True)
            yn = cen * lax.rsqrt(var + RET_GN_EPS) * gn_ref[...]
            out_ref[b, sl, :] = (_silu(g_ref[b, sl, :]) * yn).astype(out_ref.dtype)
    for b in range(nb):
        state_ref[b] = states[b]


def _ret_mix(pm, cos_t, sin_t, gn_w, tt):
    nb, seq, _ = pm.shape
    nt = seq // tt
    base = 4 * RWKV_PAIRS

    def col(off):
        return pl.BlockSpec((nb, tt, LANES), lambda h, i: (0, i, base + off + h))

    tab = pl.BlockSpec((nb, tt, LANES), lambda h, i: (0, i, 0))
    return pl.pallas_call(
        functools.partial(_ret_kernel, tt=tt, nb=nb),
        grid=(RET_HEADS, nt),
        in_specs=[col(0), col(RET_HEADS), col(2 * RET_HEADS), col(3 * RET_HEADS), tab, tab,
                  pl.BlockSpec((1, LANES), lambda h, i: (0, h))],
        out_specs=pl.BlockSpec((nb, tt, LANES), lambda h, i: (0, i, h)),
        out_shape=jax.ShapeDtypeStruct((nb, seq, RET_W), BF16),
        scratch_shapes=[pltpu.VMEM((nb, LANES, LANES), F32)],
        compiler_params=_cparams(("parallel", "arbitrary")),
        name="retention",
    )(pm, pm, pm, pm, cos_t, sin_t, gn_w.reshape(1, RET_W))


def _gdn_kernel(q_ref, k_ref, v_ref, z_ref, ab_ref, cw_ref, par_ref, nw_ref,
                out_ref, state_ref, carry_ref, *, tt, nb):
    c = CHUNK
    nc = tt // c
    head = pl.program_id(0)

    @pl.when(pl.program_id(1) == 0)
    def _():
        state_ref[...] = jnp.zeros_like(state_ref)
        carry_ref[...] = jnp.zeros_like(carry_ref)

    cw = cw_ref[...]
    par = par_ref[...]
    neg_rate = -jnp.exp(par[:, 0:1])
    dt_bias = par[:, 1:2]

    def softplus(x):
        return jnp.maximum(x, 0.0) + jnp.log(1.0 + jnp.exp(-jnp.abs(x)))

    def l2norm(x):
        return x * lax.rsqrt(jnp.sum(x * x, axis=-1, keepdims=True) + L2_EPS)

    ri = lax.broadcasted_iota(jnp.int32, (c, c), 0)
    ci_ = lax.broadcasted_iota(jnp.int32, (c, c), 1)
    causal = ci_ <= ri
    strict = ci_ < ri
    eye = jnp.where(ci_ == ri, 1.0, 0.0)
    lane = lax.broadcasted_iota(jnp.int32, (1, LANES), 1)
    sub = lax.broadcasted_iota(jnp.int32, (2 * 8, 1), 0)

    rows = []
    for b in range(nb):
        def conv_silu(ref, i):
            x = ref[b]
            prev = carry_ref[b, i]
            acc = x * cw[i, GDN_CONV - 1:GDN_CONV]
            for j in range(GDN_CONV - 1):
                acc = acc + _shift_rows(prev, x, GDN_CONV - 1 - j) * cw[i, j:j + 1]
            carry_ref[b, i] = x[tt - 8:]
            return _silu(acc)

        q = l2norm(conv_silu(q_ref, 0)) * (LANES ** -0.5)
        k = l2norm(conv_silu(k_ref, 1))
        v = conv_silu(v_ref, 2)
        ab = ab_ref[b]
        a_col = jnp.sum(jnp.where(lane == head, ab, 0.0), axis=-1, keepdims=True)
        b_col = jnp.sum(jnp.where(lane == head + GDN_HEADS, ab, 0.0), axis=-1, keepdims=True)
        ab_t = ab.T[:2 * 8]
        a_row = jnp.sum(jnp.where(sub == head, ab_t, 0.0), axis=0, keepdims=True)
        rows.append(dict(q=q, k=k, v=v,
                         g_col=neg_rate * softplus(a_col + dt_bias),
                         g_row=neg_rate * softplus(a_row + dt_bias),
                         beta=_sigmoid(b_col)))

    items = [(ci, b) for ci in range(nc) for b in range(nb)]
    pre = {}
    for ci, b in items:
        sl = slice(ci * c, (ci + 1) * c)
        rw = rows[b]
        gr = rw["g_row"][:, sl]
        gcl = rw["g_col"][sl]
        gc_col = jnp.sum(jnp.where(causal, gr, 0.0), axis=1, keepdims=True)
        gc_row = jnp.sum(jnp.where(ri <= ci_, gcl, 0.0), axis=0, keepdims=True)
        g_last = gc_col[c - 1:c]
        decay = jnp.where(causal, jnp.exp(jnp.minimum(gc_col - gc_row, 0.0)), 0.0)
        qc, kc, vc, bc = rw["q"][sl], rw["k"][sl], rw["v"][sl], rw["beta"][sl]
        kb = kc * bc
        kk_t = _dot_nt(jnp.concatenate([kb, qc], axis=0), kc)
        e_gc = jnp.exp(gc_col)
        pre[ci, b] = dict(
            lower=jnp.where(strict, kk_t[:c] * decay, 0.0),
            attn=kk_t[c:] * decay,
            vk=jnp.concatenate([vc * bc, kb * e_gc], axis=1),
            q_in=qc * e_gc,
            k_out=kc * jnp.exp(g_last - gc_col),
            p_end=jnp.exp(g_last))
    t_invs = _unit_lower_inverses([-pre[it]["lower"] for it in items], eye, lambda x: x, 5)
    for it, t_inv in zip(items, t_invs):
        pre[it]["uw"] = _dot(t_inv, pre[it]["vk"])

    states = [state_ref[b] for b in range(nb)]
    for ci in range(nc):
        sl = slice(ci * c, (ci + 1) * c)
        for b in range(nb):
            p = pre[ci, b]
            state = states[b]
            v_new = p["uw"][:, :LANES] - _dot(p["uw"][:, LANES:], state)
            o = _dot(p["q_in"], state) + _dot(p["attn"], v_new)
            states[b] = state * p["p_end"] + _dot_tn(p["k_out"], v_new)
            on = o * lax.rsqrt(jnp.mean(o * o, axis=-1, keepdims=True) + NORM_EPS)
            out_ref[b, sl, :] = (on * nw_ref[...] * _silu(z_ref[b, sl, :])).astype(out_ref.dtype)
    for b in range(nb):
        state_ref[b] = states[b]


def _gdn_mix(pm, pe, conv_w, par, norm_w, tt):
    nb, seq, _ = pm.shape
    nt = seq // tt
    base = 4 * RWKV_PAIRS + 4 * RET_HEADS

    def col(off):
        return pl.BlockSpec((nb, tt, LANES), lambda h, i: (0, i, base + off + h))

    return pl.pallas_call(
        functools.partial(_gdn_kernel, tt=tt, nb=nb),
        grid=(GDN_HEADS, nt),
        in_specs=[col(0), col(GDN_HEADS), col(2 * GDN_HEADS), col(3 * GDN_HEADS),
                  pl.BlockSpec((nb, tt, LANES), lambda h, i: (0, i, 6)),
                  pl.BlockSpec((3, GDN_CONV, LANES), lambda h, i: (0, 0, h)),
                  pl.BlockSpec((None, 1, LANES), lambda h, i: (h, 0, 0)),
                  pl.BlockSpec((1, LANES), lambda h, i: (0, h))],
        out_specs=pl.BlockSpec((nb, tt, LANES), lambda h, i: (0, i, h)),
        out_shape=jax.ShapeDtypeStruct((nb, seq, GDN_W), BF16),
        scratch_shapes=[pltpu.VMEM((nb, LANES, LANES), F32), pltpu.VMEM((nb, 3, 8, LANES), F32)],
        compiler_params=_cparams(("parallel", "arbitrary")),
        name="gated_deltanet",
    )(pm, pm, pm, pm, pe, conv_w, par, norm_w.reshape(1, GDN_W))


def _router_kernel(x_ref, nw_ref, wr_ref, br_ref, h_ref, gates_ref):
    x = x_ref[...]
    h = x * lax.rsqrt(jnp.mean(x * x, axis=-1, keepdims=True) + NORM_EPS) * nw_ref[...]
    h_ref[...] = h.astype(h_ref.dtype)
    logits = jnp.dot(h, wr_ref[...], precision=lax.Precision.HIGHEST,
                     preferred_element_type=F32) + br_ref[...]
    lane_i = lax.broadcasted_iota(jnp.int32, logits.shape, 1)
    lane = lane_i.astype(F32)
    group_of_lane = (lane_i // MOE_PER_GROUP).astype(F32)
    neg = -jnp.inf
    is_group = (lane_i >= MOE_EXPERTS) & (lane_i < MOE_EXPERTS + MOE_GROUPS)
    gl = jnp.where(is_group, logits, neg)
    gmax = jnp.max(gl, axis=-1, keepdims=True)
    gidx = jnp.min(jnp.where(gl == gmax, lane - MOE_EXPERTS, LANES), axis=-1, keepdims=True)
    group_gate = 1.0 / jnp.sum(jnp.exp(gl - gmax), axis=-1, keepdims=True)
    in_group = (lane_i < MOE_EXPERTS) & (group_of_lane == gidx)
    el = jnp.where(in_group, logits, neg)
    m1 = jnp.max(el, axis=-1, keepdims=True)
    i1 = jnp.min(jnp.where(el == m1, lane, LANES), axis=-1, keepdims=True)
    el2 = jnp.where(lane == i1, neg, el)
    m2 = jnp.max(el2, axis=-1, keepdims=True)
    i2 = jnp.min(jnp.where(el2 == m2, lane, LANES), axis=-1, keepdims=True)
    esum = jnp.sum(jnp.exp(el - m1), axis=-1, keepdims=True)
    p1 = 1.0 / esum
    p2 = jnp.exp(m2 - m1) / esum
    tot = p1 + p2
    within = jnp.where(lane == i1, p1 / tot, jnp.where(lane == i2, p2 / tot, 0.0))
    gates_ref[...] = group_gate * within


def _router(x, norm_w, wr, br, tm=512):
    t, d = x.shape
    row = lambda i: (i, 0)
    const = lambda i: (0, 0)
    return pl.pallas_call(
        _router_kernel,
        grid=(t // tm,),
        in_specs=[pl.BlockSpec((tm, d), row), pl.BlockSpec((1, d), const),
                  pl.BlockSpec((d, LANES), const), pl.BlockSpec((1, LANES), const)],
        out_specs=[pl.BlockSpec((tm, d), row), pl.BlockSpec((tm, LANES), row)],
        out_shape=[jax.ShapeDtypeStruct((t, d), BF16), jax.ShapeDtypeStruct((t, LANES), F32)],
        compiler_params=_cparams(("parallel",)),
        name="moe_router",
    )(x, norm_w.reshape(1, d), wr, br)


def _moe_dense_kernel(h_ref, g_ref, w1_ref, w3_ref, w2_ref, x_ref, o_ref, acc_ref, *, te):
    j = pl.program_id(1)

    @pl.when(j == 0)
    def _():
        acc_ref[...] = x_ref[...]

    h = h_ref[...]
    gts = g_ref[...]
    lane = lax.broadcasted_iota(jnp.int32, (1, LANES), 1)
    acc = acc_ref[...]
    for e in range(te):
        gate = jnp.sum(jnp.where(lane == j * te + e, gts, 0.0), axis=-1, keepdims=True)
        a1 = jnp.dot(h, w1_ref[e], preferred_element_type=F32)
        a3 = jnp.dot(h, w3_ref[e], preferred_element_type=F32)
        hid = _silu(a1) * a3 * gate
        acc = acc + jnp.dot(hid.astype(BF16), w2_ref[e], preferred_element_type=F32)
    acc_ref[...] = acc

    @pl.when(j == pl.num_programs(1) - 1)
    def _():
        o_ref[...] = acc


def _moe_dense(h, gates, w1, w3, w2, x, l, tm=512, te=2):
    t, d = x.shape
    ne = MOE_EXPERTS // te
    return pl.pallas_call(
        functools.partial(_moe_dense_kernel, te=te),
        grid=(t // tm, ne),
        in_specs=[pl.BlockSpec((tm, d), lambda i, j: (i, 0)),
                  pl.BlockSpec((tm, LANES), lambda i, j: (i, 0)),
                  pl.BlockSpec((None, te, d, MOE_HIDDEN), lambda i, j: (l, j, 0, 0)),
                  pl.BlockSpec((None, te, d, MOE_HIDDEN), lambda i, j: (l, j, 0, 0)),
                  pl.BlockSpec((None, te, MOE_HIDDEN, d), lambda i, j: (l, j, 0, 0)),
                  pl.BlockSpec((tm, d), lambda i, j: (i, 0))],
        out_specs=pl.BlockSpec((tm, d), lambda i, j: (i, 0)),
        out_shape=jax.ShapeDtypeStruct((t, d), F32),
        scratch_shapes=[pltpu.VMEM((tm, d), F32)],
        compiler_params=_cparams(("parallel", "arbitrary")),
        name="moe_experts",
    )(h, gates, w1, w3, w2, x)


def _pad_cols(w, n):
    return jnp.pad(w, ((0, 0), (0, n - w.shape[1])))


def _pad_rows(w, n):
    return jnp.pad(w, ((0, n - w.shape[0]), (0, 0)))


def _mixers(l, h, batch, tt, cos_t, sin_t, v_first, w_in, rwkv_mu_rkvg, rwkv_mu_wa, rwkv_w0, rwkv_w1,
            rwkv_w2, rwkv_a0, rwkv_a1, rwkv_a2, rwkv_k_k, rwkv_k_a, rwkv_r_k, rwkv_ln_w, rwkv_ln_b,
            rwkv_mu_vres, rwkv_v0, rwkv_v1, rwkv_v2, ret_gn_w, gdn_conv_w, gdn_a_log, gdn_dt_bias,
            gdn_norm_w):
    t = h.shape[0]
    seq = t // batch
    d = D_MODEL
    wt_in = w_in[l].T.astype(BF16)
    mu_w = rwkv_mu_wa[l][0][None, :]
    mu_a = rwkv_mu_wa[l][1][None, :]
    w1t, a1t = rwkv_w1[l].T, rwkv_a1[l].T
    blocks = [(1.0 - mu_w) * w1t, mu_w * w1t, (1.0 - mu_a) * a1t, mu_a * a1t]
    if l > 0:
        mu_v = rwkv_mu_vres[l - 1][None, :]
        v1t = rwkv_v1[l - 1].T
        blocks += [(1.0 - mu_v) * v1t, mu_v * v1t]
        v0 = rwkv_v0[l - 1]
        v2 = _pad_rows(rwkv_v2[l - 1], LANES).astype(BF16)
    else:
        blocks += [jnp.zeros((LANES, d), F32)] * 2
        v0 = jnp.zeros((RWKV_W,), F32)
        v2 = None
    blocks.append(w_in[l][:, N_MAIN:].T)
    wt_ext = jnp.concatenate([_pad_rows(b, LANES) for b in blocks], axis=0).astype(BF16)

    pm = _matmul_nt(h, wt_in, N_MAIN, tm=min(1024, t), tn=1024).reshape(batch, seq, N_MAIN)
    pe = _matmul_nt(h, wt_ext, N_EXT, tm=min(1024, t), tn=N_EXT).reshape(batch, seq, N_EXT)

    par = jnp.stack([rwkv_w0[l], rwkv_a0[l], v0, rwkv_k_k[l], rwkv_k_a[l], rwkv_ln_w[l], rwkv_ln_b[l],
                     rwkv_r_k[l].reshape(RWKV_W)])
    out_a, v_first = _rwkv_mix(pm, pe, rwkv_mu_rkvg[l], par,
                               _pad_rows(rwkv_w2[l], LANES).astype(BF16),
                               _pad_rows(rwkv_a2[l], LANES).astype(BF16), v2, v_first, tt)
    out_b = _ret_mix(pm, cos_t, sin_t, ret_gn_w[l], tt)
    gpar = _pad_cols(jnp.stack([gdn_a_log[l], gdn_dt_bias[l]], axis=1), LANES).reshape(GDN_HEADS, 1, LANES)
    out_c = _gdn_mix(pm, pe, gdn_conv_w[l].reshape(GDN_CONV, 3, GDN_W).transpose(1, 0, 2), gpar,
                     gdn_norm_w[l], tt)
    return (out_a.reshape(t, RWKV_W), out_b.reshape(t, RET_W), out_c.reshape(t, GDN_W), v_first)


def kernel(x, positions, norm1_w, w_in, w_out, rwkv_mu_rkvg, rwkv_mu_wa, rwkv_w0, rwkv_w1, rwkv_w2, rwkv_a0, rwkv_a1, rwkv_a2, rwkv_k_k, rwkv_k_a, rwkv_r_k, rwkv_ln_w, rwkv_ln_b, rwkv_mu_vres, rwkv_v0, rwkv_v1, rwkv_v2, ret_gn_w, gdn_conv_w, gdn_a_log, gdn_dt_bias, gdn_norm_w, norm2_w, moe_group_w, moe_group_b, moe_expert_w, moe_expert_b, moe_w1, moe_w3, moe_w2, final_norm_w):
    batch, seq, d = x.shape
    t = batch * seq
    depth = w_in.shape[0]
    tt = min(256, seq)
    tm = min(512, t)
    xf = x.reshape(t, d)
    cos_t, sin_t = _rope_tables(positions, tm)
    cos_t = cos_t.reshape(batch, seq, LANES)
    sin_t = sin_t.reshape(batch, seq, LANES)
    w1_bf, w3_bf, w2_bf = moe_w1.astype(BF16), moe_w3.astype(BF16), moe_w2.astype(BF16)
    w_out_bf = w_out.astype(BF16)
    v_first = None
    for l in range(depth):
        h = _rmsnorm(xf, norm1_w[l], BF16, tm)
        out_a, out_b, out_c, v_first = _mixers(
            l, h, batch, tt, cos_t, sin_t, v_first, w_in, rwkv_mu_rkvg, rwkv_mu_wa, rwkv_w0, rwkv_w1,
            rwkv_w2, rwkv_a0, rwkv_a1, rwkv_a2, rwkv_k_k, rwkv_k_a, rwkv_r_k, rwkv_ln_w, rwkv_ln_b,
            rwkv_mu_vres, rwkv_v0, rwkv_v1, rwkv_v2, ret_gn_w, gdn_conv_w, gdn_a_log, gdn_dt_bias,
            gdn_norm_w)
        xf = _outproj(out_a, out_b, out_c, w_out_bf, xf, l, tm)
        wr = _pad_cols(jnp.concatenate([moe_expert_w[l], moe_group_w[l]], axis=1), LANES)
        br = _pad_cols(jnp.concatenate([moe_expert_b[l], moe_group_b[l]]).reshape(1, -1), LANES)
        h2, gates = _router(xf, norm2_w[l], wr, br, tm)
        xf = _moe_dense(h2, gates, w1_bf, w3_bf, w2_bf, xf, l, tm)
    out = _rmsnorm(xf, final_norm_w, F32, tm)
    return out.reshape(batch, seq, d)
```

```python
import functools

import jax
import jax.numpy as jnp
from jax import lax
from jax.experimental import pallas as pl
from jax.experimental.pallas import tpu as pltpu

F32 = jnp.float32
BF16 = jnp.bfloat16

D_MODEL = 2048
RWKV_W = 768
RET_W = 512
GDN_W = 768
RWKV_HEAD = 64
RWKV_PAIRS = RWKV_W // 128
RET_HEADS = RET_W // 128
GDN_HEADS = GDN_W // 128
LANES = 128
RWKV_DECAY_SCALE = 0.6065306597126334
RWKV_GN_EPS = 64e-5
RET_GN_EPS = 1e-5
ROPE_BASE = 10000.0
GDN_CONV = 4
MOE_GROUPS = 4
MOE_PER_GROUP = 8
MOE_EXPERTS = 32
MOE_HIDDEN = 256
NORM_EPS = 1e-6
L2_EPS = 1e-6
N_MAIN = 4 * RWKV_W + 4 * RET_W + 4 * GDN_W
N_EXT = 7 * LANES
CHUNK = 64
RET_CHUNK = 128
VMEM_LIMIT = 56 * 1024 * 1024


def _cparams(sem):
    return pltpu.CompilerParams(dimension_semantics=sem, vmem_limit_bytes=VMEM_LIMIT)


def _dot(a, b):
    return jnp.dot(a.astype(BF16), b.astype(BF16), preferred_element_type=F32)


def _dot_nt(a, b):
    return lax.dot_general(a.astype(BF16), b.astype(BF16), (((1,), (1,)), ((), ())),
                           preferred_element_type=F32)


def _dot_tn(a, b):
    return lax.dot_general(a.astype(BF16), b.astype(BF16), (((0,), (0,)), ((), ())),
                           preferred_element_type=F32)


def _sigmoid(x):
    return 1.0 / (1.0 + jnp.exp(-x))


def _silu(x):
    return x * _sigmoid(x)


def _shift_rows(prev8, x, n):
    xs = jnp.concatenate([prev8, x], axis=0)
    return pltpu.roll(xs, n, axis=0)[8:]


def _cumsum_rows(x, tril):
    hi = x.astype(BF16)
    lo = (x - hi.astype(F32)).astype(BF16)
    return (jnp.dot(tril, hi, preferred_element_type=F32)
            + jnp.dot(tril, lo, preferred_element_type=F32))


def _unit_lower_inverses(lows, eye, expand, steps):
    xs = [eye + low for low in lows]
    ps = list(lows)
    for _ in range(steps):
        ps = [_dot(p, expand(p)) for p in ps]
        xs = [x + _dot(x, expand(p)) for x, p in zip(xs, ps)]
    return xs


def _rmsnorm_kernel(x_ref, w_ref, o_ref):
    x = x_ref[...]
    y = x * lax.rsqrt(jnp.mean(x * x, axis=-1, keepdims=True) + NORM_EPS) * w_ref[...]
    o_ref[...] = y.astype(o_ref.dtype)


def _rmsnorm(x, w, out_dtype, tm=512):
    t, d = x.shape
    return pl.pallas_call(
        _rmsnorm_kernel,
        grid=(t // tm,),
        in_specs=[pl.BlockSpec((tm, d), lambda i: (i, 0)), pl.BlockSpec((1, d), lambda i: (0, 0))],
        out_specs=pl.BlockSpec((tm, d), lambda i: (i, 0)),
        out_shape=jax.ShapeDtypeStruct((t, d), out_dtype),
        compiler_params=_cparams(("parallel",)),
        name="rmsnorm",
    )(x, w.reshape(1, d))


def _matmul_nt_kernel(a_ref, bt_ref, o_ref):
    o_ref[...] = lax.dot_general(a_ref[...], bt_ref[...], (((1,), (1,)), ((), ())),
                                 preferred_element_type=F32).astype(o_ref.dtype)


def _matmul_nt(a, bt, n, tm, tn, out_dtype=F32):
    m, k = a.shape
    return pl.pallas_call(
        _matmul_nt_kernel,
        grid=(n // tn, m // tm),
        in_specs=[pl.BlockSpec((tm, k), lambda j, i: (i, 0)), pl.BlockSpec((tn, k), lambda j, i: (j, 0))],
        out_specs=pl.BlockSpec((tm, tn), lambda j, i: (i, j)),
        out_shape=jax.ShapeDtypeStruct((m, n), out_dtype),
        compiler_params=_cparams(("parallel", "parallel")),
        name="in_proj",
    )(a, bt)


def _outproj_kernel(a_ref, b_ref, c_ref, w_ref, x_ref, o_ref):
    acc = x_ref[...]
    acc = acc + jnp.dot(a_ref[...], w_ref[:RWKV_W], preferred_element_type=F32)
    acc = acc + jnp.dot(b_ref[...], w_ref[RWKV_W:RWKV_W + RET_W], preferred_element_type=F32)
    acc = acc + jnp.dot(c_ref[...], w_ref[RWKV_W + RET_W:], preferred_element_type=F32)
    o_ref[...] = acc


def _outproj(oa, ob, oc, w_out, x, l, tm=512):
    t, d = x.shape
    row = lambda i: (i, 0)
    return pl.pallas_call(
        _outproj_kernel,
        grid=(t // tm,),
        in_specs=[pl.BlockSpec((tm, RWKV_W), row), pl.BlockSpec((tm, RET_W), row),
                  pl.BlockSpec((tm, GDN_W), row),
                  pl.BlockSpec((None, d, d), lambda i: (l, 0, 0)), pl.BlockSpec((tm, d), row)],
        out_specs=pl.BlockSpec((tm, d), row),
        out_shape=jax.ShapeDtypeStruct((t, d), F32),
        compiler_params=_cparams(("parallel",)),
        name="out_proj",
    )(oa, ob, oc, w_out, x)


def _rope_kernel(pos_ref, inv_ref, cos_ref, sin_ref):
    ang = pos_ref[...].astype(F32) * inv_ref[...]
    lane = lax.broadcasted_iota(jnp.int32, ang.shape, 1)
    s = jnp.sin(ang)
    cos_ref[...] = jnp.cos(ang)
    sin_ref[...] = jnp.where(lane < LANES // 2, -s, s)


def _rope_tables(positions, tm=512):
    t = positions.size
    inv = ROPE_BASE ** (-jnp.arange(0, LANES, 2, dtype=F32) / LANES)
    inv = jnp.concatenate([inv, inv]).reshape(1, LANES)
    tm = min(tm, t)
    return pl.pallas_call(
        _rope_kernel,
        grid=(t // tm,),
        in_specs=[pl.BlockSpec((tm, 1), lambda i: (i, 0)), pl.BlockSpec((1, LANES), lambda i: (0, 0))],
        out_specs=[pl.BlockSpec((tm, LANES), lambda i: (i, 0))] * 2,
        out_shape=[jax.ShapeDtypeStruct((t, LANES), F32)] * 2,
        compiler_params=_cparams(("parallel",)),
        name="rope_tables",
    )(positions.reshape(t, 1), inv)


def _rwkv_kernel(*refs, tt, nb, npp, has_vres):
    if has_vres:
        (r_ref, k_ref, v_ref, g_ref, ext_ref, mu_ref, par_ref, w2_ref, a2_ref, v2_ref, vf_ref,
         out_ref, state_ref, carry_ref, ecarry_ref) = refs
    else:
        (r_ref, k_ref, v_ref, g_ref, ext_ref, mu_ref, par_ref, w2_ref, a2_ref,
         out_ref, vf_out_ref, state_ref, carry_ref, ecarry_ref) = refs
    c = CHUNK
    hd = RWKV_HEAD
    nc = tt // c

    @pl.when(pl.program_id(1) == 0)
    def _():
        state_ref[...] = jnp.zeros_like(state_ref)
        carry_ref[...] = jnp.zeros_like(carry_ref)
        ecarry_ref[...] = jnp.zeros_like(ecarry_ref)

    lane = lax.broadcasted_iota(jnp.int32, (1, LANES), 1)
    first = lane < hd

    def head_sum(x):
        s1 = jnp.sum(jnp.where(first, x, 0.0), axis=-1, keepdims=True)
        s2 = jnp.sum(jnp.where(first, 0.0, x), axis=-1, keepdims=True)
        return jnp.where(first, s1, s2)

    def expand(x):
        return jnp.concatenate([jnp.where(first, x, 0.0), jnp.where(first, 0.0, x)], axis=0)

    mu = mu_ref[...]
    par = par_ref[...]

    row = lax.broadcasted_iota(jnp.int32, (c, 2 * c), 0)
    col = lax.broadcasted_iota(jnp.int32, (c, 2 * c), 1) & (c - 1)
    strict = col < row
    incl = col <= row
    eye = jnp.where(col == row, 1.0, 0.0)
    tri_r = lax.broadcasted_iota(jnp.int32, (c, c), 0)
    tri_c = lax.broadcasted_iota(jnp.int32, (c, c), 1)
    tril = jnp.where(tri_c <= tri_r, 1.0, 0.0).astype(BF16)
    srow = lax.broadcasted_iota(jnp.int32, (LANES, LANES), 0)
    scol = lax.broadcasted_iota(jnp.int32, (LANES, LANES), 1)
    same_head = (srow < hd) == (scol < hd)

    rows = {}
    for b in range(nb):
        def mixed(ref, i):
            x = ref[b]
            xs = _shift_rows(carry_ref[b, i], x, 1)
            carry_ref[b, i] = x[tt - 8:]
            return x + (xs - x) * mu[i:i + 1]

        r_all = mixed(r_ref, 0)
        k_all = mixed(k_ref, 1)
        v_all = mixed(v_ref, 2)
        g_all = mixed(g_ref, 3)
        ext = ext_ref[b]
        exts = _shift_rows(ecarry_ref[b], ext, 1)
        ecarry_ref[b] = ext[tt - 8:]

        def low_rank(i):
            return (ext[:, 2 * i * LANES:(2 * i + 1) * LANES]
                    + exts[:, (2 * i + 1) * LANES:(2 * i + 2) * LANES])

        lr_w = jnp.tanh(low_rank(0))
        lr_a = low_rank(1)
        lr_v = low_rank(2) if has_vres else None
        for pi in range(npp):
            ls = slice(pi * LANES, (pi + 1) * LANES)
            w0, a0, v0, k_k, k_a, ln_w, ln_b, r_k = (par[i:i + 1, ls] for i in range(8))
            r, k, v, g = r_all[:, ls], k_all[:, ls], v_all[:, ls], g_all[:, ls]
            log_w = -RWKV_DECAY_SCALE * _sigmoid(w0 + _dot(lr_w, w2_ref[:, ls]))
            a = _sigmoid(a0 + _dot(lr_a, a2_ref[:, ls]))
            if has_vres:
                v = v + (vf_ref[b, :, ls] - v) * _sigmoid(v0 + _dot(lr_v, v2_ref[:, ls]))
            else:
                vf_out_ref[b, :, ls] = v
            kk = k * k_k
            kk = kk * lax.rsqrt(head_sum(kk * kk) + L2_EPS)
            k = k * (1.0 + (a - 1.0) * k_a)
            bonus = head_sum(r * k * r_k) * v
            rows[b, pi] = dict(r=r, k=k, v=v, g=g, log_w=log_w, kk=kk, kka=kk * a, bonus=bonus,
                               ln_w=ln_w, ln_b=ln_b)

    items = [(ci, b, pi) for ci in range(nc) for b in range(nb) for pi in range(npp)]
    pre = {}
    for ci, b, pi in items:
        sl = slice(ci * c, (ci + 1) * c)
        rw = rows[b, pi]
        lw = rw["log_w"][sl]
        gc = _cumsum_rows(lw, tril)
        g_last = gc[c - 1:c]
        dec_inv = jnp.exp(-gc)
        dec_out = jnp.exp(g_last - gc)
        a_t = -rw["kk"][sl] * jnp.exp(gc - lw)
        r_t = rw["r"][sl] * jnp.exp(gc)
        b_t = rw["kka"][sl] * dec_inv
        k_t = rw["k"][sl] * dec_inv
        sc = _dot_nt(jnp.concatenate([a_t, r_t], axis=0),
                     jnp.concatenate([expand(b_t), expand(k_t)], axis=0))
        pre[ci, b, pi] = dict(
            a_t=a_t, r_t=r_t, v=rw["v"][sl],
            l_ab=jnp.where(strict, sc[:c, :2 * c], 0.0),
            l_ak=jnp.where(strict, sc[:c, 2 * c:], 0.0),
            m_rb=jnp.where(incl, sc[c:, :2 * c], 0.0),
            m_rk=jnp.where(incl, sc[c:, 2 * c:], 0.0),
            kd=jnp.concatenate([rw["kka"][sl] * dec_out, rw["k"][sl] * dec_out], axis=0),
            p_end=jnp.exp(g_last))
    t_invs = _unit_lower_inverses([pre[it]["l_ab"] for it in items], eye, expand, 5)
    for it, t_inv in zip(items, t_invs):
        p = pre[it]
        ev = expand(p["v"])
        wu = _dot(t_inv, jnp.concatenate([expand(p["a_t"]), expand(_dot(p["l_ak"], ev))], axis=1))
        p["w_t"] = wu[:, :LANES]
        p["u0"] = wu[:, LANES:]
        p["o0"] = _dot(p["m_rk"], ev)

    keys = [(b, pi) for b in range(nb) for pi in range(npp)]
    states = {key: state_ref[key] for key in keys}
    for ci in range(nc):
        sl = slice(ci * c, (ci + 1) * c)
        for b, pi in keys:
            p = pre[ci, b, pi]
            rw = rows[b, pi]
            state = states[b, pi]
            u = p["u0"] + _dot_nt(p["w_t"], state)
            o = p["o0"] + _dot_nt(p["r_t"], state) + _dot(p["m_rb"], expand(u))
            upd = _dot_tn(jnp.concatenate([u, p["v"]], axis=0), p["kd"])
            states[b, pi] = state * p["p_end"] + jnp.where(same_head, upd, 0.0)

            mean = head_sum(o) * (1.0 / hd)
            cen = o - mean
            var = head_sum(cen * cen) * (1.0 / hd)
            y = cen * lax.rsqrt(var + RWKV_GN_EPS) * rw["ln_w"] + rw["ln_b"]
            out_ref[b, sl, pi * LANES:(pi + 1) * LANES] = (
                (y + rw["bonus"][sl]) * _sigmoid(rw["g"][sl])).astype(out_ref.dtype)
    for key in keys:
        state_ref[key] = states[key]


def _rwkv_mix(pm, pe, mu_rkvg, par, w2, a2, v2, v_first, tt):
    nb, seq, _ = pm.shape
    nt = seq // tt
    has_vres = v_first is not None
    npp = 2
    ngrp = RWKV_PAIRS // npp
    wide = npp * LANES

    def col(off):
        return pl.BlockSpec((nb, tt, wide), lambda p, i: (0, i, off + p))

    pcol = lambda nrows: pl.BlockSpec((nrows, wide), lambda p, i: (0, p))
    in_specs = [col(0), col(ngrp), col(2 * ngrp), col(3 * ngrp),
                pl.BlockSpec((nb, tt, 6 * LANES), lambda p, i: (0, i, 0)),
                pcol(4), pcol(8), pcol(LANES), pcol(LANES)]
    args = [pm, pm, pm, pm, pe, mu_rkvg, par, w2, a2]
    out_block = pl.BlockSpec((nb, tt, wide), lambda p, i: (0, i, p))
    if has_vres:
        in_specs += [pcol(LANES), out_block]
        args += [v2, v_first]
        out_specs = out_block
        out_shape = jax.ShapeDtypeStruct((nb, seq, RWKV_W), BF16)
    else:
        out_specs = [out_block, out_block]
        out_shape = [jax.ShapeDtypeStruct((nb, seq, RWKV_W), BF16),
                     jax.ShapeDtypeStruct((nb, seq, RWKV_W), F32)]
    res = pl.pallas_call(
        functools.partial(_rwkv_kernel, tt=tt, nb=nb, npp=npp, has_vres=has_vres),
        grid=(ngrp, nt),
        in_specs=in_specs,
        out_specs=out_specs,
        out_shape=out_shape,
        scratch_shapes=[pltpu.VMEM((nb, npp, LANES, LANES), F32), pltpu.VMEM((nb, 4, 8, wide), F32),
                        pltpu.VMEM((nb, 8, 6 * LANES), F32)],
        compiler_params=_cparams(("parallel", "arbitrary")),
        name="rwkv7_vres" if has_vres else "rwkv7",
    )(*args)
    if has_vres:
        return res, v_first
    return res[0], res[1]


def _ret_kernel(q_ref, k_ref, v_ref, g_ref, cos_ref, sin_ref, gn_ref, out_ref, state_ref, *, tt, nb):
    c = RET_CHUNK

    @pl.when(pl.program_id(1) == 0)
    def _():
        state_ref[...] = jnp.zeros_like(state_ref)

    head = pl.program_id(0).astype(F32)
    log_gamma = jnp.log(1.0 - jnp.exp2(jnp.full((1, 1), -5.0, F32) - head))
    ri = lax.broadcasted_iota(jnp.int32, (c, c), 0)
    ci_ = lax.broadcasted_iota(jnp.int32, (c, c), 1)
    diff = (ri - ci_).astype(F32)
    dmask = jnp.where(diff >= 0, jnp.exp(jnp.maximum(diff, 0.0) * log_gamma), 0.0)
    idx = lax.broadcasted_iota(jnp.int32, (c, 1), 0).astype(F32)
    k_dec = jnp.exp((c - 1.0 - idx) * log_gamma)
    q_dec = jnp.exp((idx + 1.0) * log_gamma)
    chunk_decay = jnp.exp(c * log_gamma)

    qs, ks = [], []
    for b in range(nb):
        cos = cos_ref[b]
        sin = sin_ref[b]

        def rotary(x):
            return x * cos + pltpu.roll(x, LANES // 2, axis=1) * sin

        qs.append(rotary(q_ref[b]))
        ks.append(rotary(k_ref[b]) * (LANES ** -0.5))

    states = [state_ref[b] for b in range(nb)]
    for i in range(tt // c):
        sl = slice(i * c, (i + 1) * c)
        for b in range(nb):
            qc, kc, vc = qs[b][sl], ks[b][sl], v_ref[b, sl, :]
            scores = _dot_nt(qc, kc) * dmask
            y = _dot(scores, vc) + _dot(qc * q_dec, states[b])
            states[b] = states[b] * chunk_decay + _dot_tn(kc * k_dec, vc)
            mean = jnp.mean(y, axis=-1, keepdims=True)
            cen = y - mean
            var = jnp.mean(cen * cen, axis=-1, keepdims=True)
            yn = cen * lax.rsqrt(var + RET_GN_EPS) * gn_ref[...]
            out_ref[b, sl, :] = (_silu(g_ref[b, sl, :]) * yn).astype(out_ref.dtype)
    for b in range(nb):
        state_ref[b] = states[b]


def _ret_mix(pm, cos_t, sin_t, gn_w, tt):
    nb, seq, _ = pm.shape
    nt = seq // tt
    base = 4 * RWKV_PAIRS

    def col(off):
        return pl.BlockSpec((nb, tt, LANES), lambda h, i: (0, i, base + off + h))

    tab = pl.BlockSpec((nb, tt, LANES), lambda h, i: (0, i, 0))
    return pl.pallas_call(
        functools.partial(_ret_kernel, tt=tt, nb=nb),
        grid=(RET_HEADS, nt),
        in_specs=[col(0), col(RET_HEADS), col(2 * RET_HEADS), col(3 * RET_HEADS), tab, tab,
                  pl.BlockSpec((1, LANES), lambda h, i: (0, h))],
        out_specs=pl.BlockSpec((nb, tt, LANES), lambda h, i: (0, i, h)),
        out_shape=jax.ShapeDtypeStruct((nb, seq, RET_W), BF16),
        scratch_shapes=[pltpu.VMEM((nb, LANES, LANES), F32)],
        compiler_params=_cparams(("parallel", "arbitrary")),
        name="retention",
    )(pm, pm, pm, pm, cos_t, sin_t, gn_w.reshape(1, RET_W))


def _gdn_kernel(q_ref, k_ref, v_ref, z_ref, ab_ref, cw_ref, par_ref, nw_ref,
                out_ref, state_ref, carry_ref, *, tt, nb):
    c = CHUNK
    nc = tt // c
    head = pl.program_id(0)

    @pl.when(pl.program_id(1) == 0)
    def _():
        state_ref[...] = jnp.zeros_like(state_ref)
        carry_ref[...] = jnp.zeros_like(carry_ref)

    cw = cw_ref[...]
    par = par_ref[...]
    neg_rate = -jnp.exp(par[:, 0:1])
    dt_bias = par[:, 1:2]

    def softplus(x):
        return jnp.maximum(x, 0.0) + jnp.log(1.0 + jnp.exp(-jnp.abs(x)))

    def l2norm(x):
        return x * lax.rsqrt(jnp.sum(x * x, axis=-1, keepdims=True) + L2_EPS)

    ri = lax.broadcasted_iota(jnp.int32, (c, c), 0)
    ci_ = lax.broadcasted_iota(jnp.int32, (c, c), 1)
    causal = ci_ <= ri
    strict = ci_ < ri
    eye = jnp.where(ci_ == ri, 1.0, 0.0)
    lane = lax.broadcasted_iota(jnp.int32, (1, LANES), 1)
    sub = lax.broadcasted_iota(jnp.int32, (2 * 8, 1), 0)

    rows = []
    for b in range(nb):
        def conv_silu(ref, i):
            x = ref[b]
            prev = carry_ref[b, i]
            acc = x * cw[i, GDN_CONV - 1:GDN_CONV]
            for j in range(GDN_CONV - 1):
                acc = acc + _shift_rows(prev, x, GDN_CONV - 1 - j) * cw[i, j:j + 1]
            carry_ref[b, i] = x[tt - 8:]
            return _silu(acc)

        q = l2norm(conv_silu(q_ref, 0)) * (LANES ** -0.5)
        k = l2norm(conv_silu(k_ref, 1))
        v = conv_silu(v_ref, 2)
        ab = ab_ref[b]
        a_col = jnp.sum(jnp.where(lane == head, ab, 0.0), axis=-1, keepdims=True)
        b_col = jnp.sum(jnp.where(lane == head + GDN_HEADS, ab, 0.0), axis=-1, keepdims=True)
        ab_t = ab.T[:2 * 8]
        a_row = jnp.sum(jnp.where(sub == head, ab_t, 0.0), axis=0, keepdims=True)
        rows.append(dict(q=q, k=k, v=v,
                         g_col=neg_rate * softplus(a_col + dt_bias),
                         g_row=neg_rate * softplus(a_row + dt_bias),
                         beta=_sigmoid(b_col)))

    items = [(ci, b) for ci in range(nc) for b in range(nb)]
    pre = {}
    for ci, b in items:
        sl = slice(ci * c, (ci + 1) * c)
        rw = rows[b]
        gr = rw["g_row"][:, sl]
        gcl = rw["g_col"][sl]
        gc_col = jnp.sum(jnp.where(causal, gr, 0.0), axis=1, keepdims=True)
        gc_row = jnp.sum(jnp.where(ri <= ci_, gcl, 0.0), axis=0, keepdims=True)
        g_last = gc_col[c - 1:c]
        decay = jnp.where(causal, jnp.exp(jnp.minimum(gc_col - gc_row, 0.0)), 0.0)
        qc, kc, vc, bc = rw["q"][sl], rw["k"][sl], rw["v"][sl], rw["beta"][sl]
        kb = kc * bc
        kk_t = _dot_nt(jnp.concatenate([kb, qc], axis=0), kc)
        e_gc = jnp.exp(gc_col)
        pre[ci, b] = dict(
            lower=jnp.where(strict, kk_t[:c] * decay, 0.0),
            attn=kk_t[c:] * decay,
            vk=jnp.concatenate([vc * bc, kb * e_gc], axis=1),
            q_in=qc * e_gc,
            k_out=kc * jnp.exp(g_last - gc_col),
            p_end=jnp.exp(g_last))
    t_invs = _unit_lower_inverses([-pre[it]["lower"] for it in items], eye, lambda x: x, 5)
    for it, t_inv in zip(items, t_invs):
        pre[it]["uw"] = _dot(t_inv, pre[it]["vk"])

    states = [state_ref[b] for b in range(nb)]
    for ci in range(nc):
        sl = slice(ci * c, (ci + 1) * c)
        for b in range(nb):
            p = pre[ci, b]
            state = states[b]
            v_new = p["uw"][:, :LANES] - _dot(p["uw"][:, LANES:], state)
            o = _dot(p["q_in"], state) + _dot(p["attn"], v_new)
            states[b] = state * p["p_end"] + _dot_tn(p["k_out"], v_new)
            on = o * lax.rsqrt(jnp.mean(o * o, axis=-1, keepdims=True) + NORM_EPS)
            out_ref[b, sl, :] = (on * nw_ref[...] * _silu(z_ref[b, sl, :])).astype(out_ref.dtype)
    for b in range(nb):
        state_ref[b] = states[b]


def _gdn_mix(pm, pe, conv_w, par, norm_w, tt):
    nb, seq, _ = pm.shape
    nt = seq // tt
    base = 4 * RWKV_PAIRS + 4 * RET_HEADS

    def col(off):
        return pl.BlockSpec((nb, tt, LANES), lambda h, i: (0, i, base + off + h))

    return pl.pallas_call(
        functools.partial(_gdn_kernel, tt=tt, nb=nb),
        grid=(GDN_HEADS, nt),
        in_specs=[col(0), col(GDN_HEADS), col(2 * GDN_HEADS), col(3 * GDN_HEADS),
                  pl.BlockSpec((nb, tt, LANES), lambda h, i: (0, i, 6)),
                  pl.BlockSpec((3, GDN_CONV, LANES), lambda h, i: (0, 0, h)),
                  pl.BlockSpec((None, 1, LANES), lambda h, i: (h, 0, 0)),
                  pl.BlockSpec((1, LANES), lambda h, i: (0, h))],
        out_specs=pl.BlockSpec((nb, tt, LANES), lambda h, i: (0, i, h)),
        out_shape=jax.ShapeDtypeStruct((nb, seq, GDN_W), BF16),
        scratch_shapes=[pltpu.VMEM((nb, LANES, LANES), F32), pltpu.VMEM((nb, 3, 8, LANES), F32)],
        compiler_params=_cparams(("parallel", "arbitrary")),
        name="gated_deltanet",
    )(pm, pm, pm, pm, pe, conv_w, par, norm_w.reshape(1, GDN_W))


def _router_kernel(x_ref, nw_ref, wr_ref, br_ref, h_ref, gates_ref):
    x = x_ref[...]
    h = x * lax.rsqrt(jnp.mean(x * x, axis=-1, keepdims=True) + NORM_EPS) * nw_ref[...]
    h_ref[...] = h.astype(h_ref.dtype)
    logits = jnp.dot(h, wr_ref[...], precision=lax.Precision.HIGHEST,
                     preferred_element_type=F32) + br_ref[...]
    lane_i = lax.broadcasted_iota(jnp.int32, logits.shape, 1)
    lane = lane_i.astype(F32)
    group_of_lane = (lane_i // MOE_PER_GROUP).astype(F32)
    neg = -jnp.inf
    is_group = (lane_i >= MOE_EXPERTS) & (lane_i < MOE_EXPERTS + MOE_GROUPS)
    gl = jnp.where(is_group, logits, neg)
    gmax = jnp.max(gl, axis=-1, keepdims=True)
    gidx = jnp.min(jnp.where(gl == gmax, lane - MOE_EXPERTS, LANES), axis=-1, keepdims=True)
    group_gate = 1.0 / jnp.sum(jnp.exp(gl - gmax), axis=-1, keepdims=True)
    in_group = (lane_i < MOE_EXPERTS) & (group_of_lane == gidx)
    el = jnp.where(in_group, logits, neg)
    m1 = jnp.max(el, axis=-1, keepdims=True)
    i1 = jnp.min(jnp.where(el == m1, lane, LANES), axis=-1, keepdims=True)
    el2 = jnp.where(lane == i1, neg, el)
    m2 = jnp.max(el2, axis=-1, keepdims=True)
    i2 = jnp.min(jnp.where(el2 == m2, lane, LANES), axis=-1, keepdims=True)
    esum = jnp.sum(jnp.exp(el - m1), axis=-1, keepdims=True)
    p1 = 1.0 / esum
    p2 = jnp.exp(m2 - m1) / esum
    tot = p1 + p2
    within = jnp.where(lane == i1, p1 / tot, jnp.where(lane == i2, p2 / tot, 0.0))
    gates_ref[...] = group_gate * within


def _router(x, norm_w, wr, br, tm=512):
    t, d = x.shape
    row = lambda i: (i, 0)
    const = lambda i: (0, 0)
    return pl.pallas_call(
        _router_kernel,
        grid=(t // tm,),
        in_specs=[pl.BlockSpec((tm, d), row), pl.BlockSpec((1, d), const),
                  pl.BlockSpec((d, LANES), const), pl.BlockSpec((1, LANES), const)],
        out_specs=[pl.BlockSpec((tm, d), row), pl.BlockSpec((tm, LANES), row)],
        out_shape=[jax.ShapeDtypeStruct((t, d), BF16), jax.ShapeDtypeStruct((t, LANES), F32)],
        compiler_params=_cparams(("parallel",)),
        name="moe_router",
    )(x, norm_w.reshape(1, d), wr, br)


def _moe_dense_kernel(h_ref, g_ref, w1_ref, w3_ref, w2_ref, x_ref, o_ref, *, te):
    j = pl.program_id(1)

    @pl.when(j == 0)
    def _():
        o_ref[...] = x_ref[...]

    h = h_ref[...]
    gts = g_ref[...]
    lane = lax.broadcasted_iota(jnp.int32, (1, LANES), 1)
    acc = o_ref[...]
    for e in range(te):
        gate = jnp.sum(jnp.where(lane == j * te + e, gts, 0.0), axis=-1, keepdims=True)
        a1 = jnp.dot(h, w1_ref[e], preferred_element_type=F32)
        a3 = jnp.dot(h, w3_ref[e], preferred_element_type=F32)
        hid = _silu(a1) * a3 * gate
        acc = acc + jnp.dot(hid.astype(BF16), w2_ref[e], preferred_element_type=F32)
    o_ref[...] = acc


def _moe_dense(h, gates, w1, w3, w2, x, l, tm=512, te=4):
    t, d = x.shape
    ne = MOE_EXPERTS // te
    return pl.pallas_call(
        functools.partial(_moe_dense_kernel, te=te),
        grid=(t // tm, ne),
        in_specs=[pl.BlockSpec((tm, d), lambda i, j: (i, 0)),
                  pl.BlockSpec((tm, LANES), lambda i, j: (i, 0)),
                  pl.BlockSpec((None, te, d, MOE_HIDDEN), lambda i, j: (l, j, 0, 0)),
                  pl.BlockSpec((None, te, d, MOE_HIDDEN), lambda i, j: (l, j, 0, 0)),
                  pl.BlockSpec((None, te, MOE_HIDDEN, d), lambda i, j: (l, j, 0, 0)),
                  pl.BlockSpec((tm, d), lambda i, j: (i, 0))],
        out_specs=pl.BlockSpec((tm, d), lambda i, j: (i, 0)),
        out_shape=jax.ShapeDtypeStruct((t, d), F32),
        compiler_params=_cparams(("parallel", "arbitrary")),
        name="moe_experts",
    )(h, gates, w1, w3, w2, x)


def _pad_cols(w, n):
    return jnp.pad(w, ((0, 0), (0, n - w.shape[1])))


def _pad_rows(w, n):
    return jnp.pad(w, ((0, n - w.shape[0]), (0, 0)))


def _mixers(l, h, batch, tt, cos_t, sin_t, v_first, w_in, rwkv_mu_rkvg, rwkv_mu_wa, rwkv_w0, rwkv_w1,
            rwkv_w2, rwkv_a0, rwkv_a1, rwkv_a2, rwkv_k_k, rwkv_k_a, rwkv_r_k, rwkv_ln_w, rwkv_ln_b,
            rwkv_mu_vres, rwkv_v0, rwkv_v1, rwkv_v2, ret_gn_w, gdn_conv_w, gdn_a_log, gdn_dt_bias,
            gdn_norm_w):
    t = h.shape[0]
    seq = t // batch
    d = D_MODEL
    wt_in = w_in[l].T.astype(BF16)
    mu_w = rwkv_mu_wa[l][0][None, :]
    mu_a = rwkv_mu_wa[l][1][None, :]
    w1t, a1t = rwkv_w1[l].T, rwkv_a1[l].T
    blocks = [(1.0 - mu_w) * w1t, mu_w * w1t, (1.0 - mu_a) * a1t, mu_a * a1t]
    if l > 0:
        mu_v = rwkv_mu_vres[l - 1][None, :]
        v1t = rwkv_v1[l - 1].T
        blocks += [(1.0 - mu_v) * v1t, mu_v * v1t]
        v0 = rwkv_v0[l - 1]
        v2 = _pad_rows(rwkv_v2[l - 1], LANES).astype(BF16)
    else:
        blocks += [jnp.zeros((LANES, d), F32)] * 2
        v0 = jnp.zeros((RWKV_W,), F32)
        v2 = None
    blocks.append(w_in[l][:, N_MAIN:].T)
    wt_ext = jnp.concatenate([_pad_rows(b, LANES) for b in blocks], axis=0).astype(BF16)

    pm = _matmul_nt(h, wt_in, N_MAIN, tm=min(1024, t), tn=1024).reshape(batch, seq, N_MAIN)
    pe = _matmul_nt(h, wt_ext, N_EXT, tm=min(1024, t), tn=N_EXT).reshape(batch, seq, N_EXT)

    par = jnp.stack([rwkv_w0[l], rwkv_a0[l], v0, rwkv_k_k[l], rwkv_k_a[l], rwkv_ln_w[l], rwkv_ln_b[l],
                     rwkv_r_k[l].reshape(RWKV_W)])
    out_a, v_first = _rwkv_mix(pm, pe, rwkv_mu_rkvg[l], par,
                               _pad_rows(rwkv_w2[l], LANES).astype(BF16),
                               _pad_rows(rwkv_a2[l], LANES).astype(BF16), v2, v_first, tt)
    out_b = _ret_mix(pm, cos_t, sin_t, ret_gn_w[l], tt)
    gpar = _pad_cols(jnp.stack([gdn_a_log[l], gdn_dt_bias[l]], axis=1), LANES).reshape(GDN_HEADS, 1, LANES)
    out_c = _gdn_mix(pm, pe, gdn_conv_w[l].reshape(GDN_CONV, 3, GDN_W).transpose(1, 0, 2), gpar,
                     gdn_norm_w[l], tt)
    return (out_a.reshape(t, RWKV_W), out_b.reshape(t, RET_W), out_c.reshape(t, GDN_W), v_first)


def kernel(x, positions, norm1_w, w_in, w_out, rwkv_mu_rkvg, rwkv_mu_wa, rwkv_w0, rwkv_w1, rwkv_w2, rwkv_a0, rwkv_a1, rwkv_a2, rwkv_k_k, rwkv_k_a, rwkv_r_k, rwkv_ln_w, rwkv_ln_b, rwkv_mu_vres, rwkv_v0, rwkv_v1, rwkv_v2, ret_gn_w, gdn_conv_w, gdn_a_log, gdn_dt_bias, gdn_norm_w, norm2_w, moe_group_w, moe_group_b, moe_expert_w, moe_expert_b, moe_w1, moe_w3, moe_w2, final_norm_w):
    batch, seq, d = x.shape
    t = batch * seq
    depth = w_in.shape[0]
    tt = min(256, seq)
    tm = min(512, t)
    xf = x.reshape(t, d)
    cos_t, sin_t = _rope_tables(positions, tm)
    cos_t = cos_t.reshape(batch, seq, LANES)
    sin_t = sin_t.reshape(batch, seq, LANES)
    w1_bf, w3_bf, w2_bf = moe_w1.astype(BF16), moe_w3.astype(BF16), moe_w2.astype(BF16)
    w_out_bf = w_out.astype(BF16)
    v_first = None
    for l in range(depth):
        h = _rmsnorm(xf, norm1_w[l], BF16, tm)
        out_a, out_b, out_c, v_first = _mixers(
            l, h, batch, tt, cos_t, sin_t, v_first, w_in, rwkv_mu_rkvg, rwkv_mu_wa, rwkv_w0, rwkv_w1,
            rwkv_w2, rwkv_a0, rwkv_a1, rwkv_a2, rwkv_k_k, rwkv_k_a, rwkv_r_k, rwkv_ln_w, rwkv_ln_b,
            rwkv_mu_vres, rwkv_v0, rwkv_v1, rwkv_v2, ret_gn_w, gdn_conv_w, gdn_a_log, gdn_dt_bias,
            gdn_norm_w)
        xf = _outproj(out_a, out_b, out_c, w_out_bf, xf, l, tm)
        wr = _pad_cols(jnp.concatenate([moe_expert_w[l], moe_group_w[l]], axis=1), LANES)
        br = _pad_cols(jnp.concatenate([moe_expert_b[l], moe_group_b[l]]).reshape(1, -1), LANES)
        h2, gates = _router(xf, norm2_w[l], wr, br, tm)
        xf = _moe_dense(h2, gates, w1_bf, w3_bf, w2_bf, xf, l, tm)
    out = _rmsnorm(xf, final_norm_w, F32, tm)
    return out.reshape(batch, seq, d)
```

```python
import functools

import jax
import jax.numpy as jnp
from jax import lax
from jax.experimental import pallas as pl
from jax.experimental.pallas import tpu as pltpu

F32 = jnp.float32
BF16 = jnp.bfloat16

D_MODEL = 2048
RWKV_W = 768
RET_W = 512
GDN_W = 768
RWKV_HEAD = 64
RWKV_PAIRS = RWKV_W // 128
RET_HEADS = RET_W // 128
GDN_HEADS = GDN_W // 128
LANES = 128
RWKV_DECAY_SCALE = 0.6065306597126334
RWKV_GN_EPS = 64e-5
RET_GN_EPS = 1e-5
ROPE_BASE = 10000.0
GDN_CONV = 4
MOE_GROUPS = 4
MOE_PER_GROUP = 8
MOE_EXPERTS = 32
MOE_HIDDEN = 256
NORM_EPS = 1e-6
L2_EPS = 1e-6
N_MAIN = 4 * RWKV_W + 4 * RET_W + 4 * GDN_W
N_EXT = 7 * LANES
CHUNK = 64
RET_CHUNK = 128
VMEM_LIMIT = 56 * 1024 * 1024


def _cparams(sem):
    return pltpu.CompilerParams(dimension_semantics=sem, vmem_limit_bytes=VMEM_LIMIT)


def _dot(a, b):
    return jnp.dot(a.astype(BF16), b.astype(BF16), preferred_element_type=F32)


def _dot_nt(a, b):
    return lax.dot_general(a.astype(BF16), b.astype(BF16), (((1,), (1,)), ((), ())),
                           preferred_element_type=F32)


def _dot_tn(a, b):
    return lax.dot_general(a.astype(BF16), b.astype(BF16), (((0,), (0,)), ((), ())),
                           preferred_element_type=F32)


def _sigmoid(x):
    return 1.0 / (1.0 + jnp.exp(-x))


def _silu(x):
    return x * _sigmoid(x)


def _shift_rows(prev8, x, n):
    xs = jnp.concatenate([prev8, x], axis=0)
    return pltpu.roll(xs, n, axis=0)[8:]


def _cumsum_rows(x, tril):
    hi = x.astype(BF16)
    lo = (x - hi.astype(F32)).astype(BF16)
    return (jnp.dot(tril, hi, preferred_element_type=F32)
            + jnp.dot(tril, lo, preferred_element_type=F32))


def _unit_lower_inverses(lows, eye, expand, steps):
    xs = [eye + low for low in lows]
    ps = list(lows)
    for _ in range(steps):
        ps = [_dot(p, expand(p)) for p in ps]
        xs = [x + _dot(x, expand(p)) for x, p in zip(xs, ps)]
    return xs


def _rmsnorm_kernel(x_ref, w_ref, o_ref):
    x = x_ref[...]
    y = x * lax.rsqrt(jnp.mean(x * x, axis=-1, keepdims=True) + NORM_EPS) * w_ref[...]
    o_ref[...] = y.astype(o_ref.dtype)


def _rmsnorm(x, w, out_dtype, tm=512):
    t, d = x.shape
    return pl.pallas_call(
        _rmsnorm_kernel,
        grid=(t // tm,),
        in_specs=[pl.BlockSpec((tm, d), lambda i: (i, 0)), pl.BlockSpec((1, d), lambda i: (0, 0))],
        out_specs=pl.BlockSpec((tm, d), lambda i: (i, 0)),
        out_shape=jax.ShapeDtypeStruct((t, d), out_dtype),
        compiler_params=_cparams(("parallel",)),
        name="rmsnorm",
    )(x, w.reshape(1, d))


def _matmul_nt_kernel(a_ref, bt_ref, o_ref):
    o_ref[...] = lax.dot_general(a_ref[...], bt_ref[...], (((1,), (1,)), ((), ())),
                                 preferred_element_type=F32).astype(o_ref.dtype)


def _matmul_nt(a, bt, n, tm, tn, out_dtype=F32):
    m, k = a.shape
    return pl.pallas_call(
        _matmul_nt_kernel,
        grid=(n // tn, m // tm),
        in_specs=[pl.BlockSpec((tm, k), lambda j, i: (i, 0)), pl.BlockSpec((tn, k), lambda j, i: (j, 0))],
        out_specs=pl.BlockSpec((tm, tn), lambda j, i: (i, j)),
        out_shape=jax.ShapeDtypeStruct((m, n), out_dtype),
        compiler_params=_cparams(("parallel", "parallel")),
        name="in_proj",
    )(a, bt)


def _outproj_kernel(a_ref, b_ref, c_ref, w_ref, x_ref, o_ref):
    acc = x_ref[...]
    acc = acc + jnp.dot(a_ref[...], w_ref[:RWKV_W], preferred_element_type=F32)
    acc = acc + jnp.dot(b_ref[...], w_ref[RWKV_W:RWKV_W + RET_W], preferred_element_type=F32)
    acc = acc + jnp.dot(c_ref[...], w_ref[RWKV_W + RET_W:], preferred_element_type=F32)
    o_ref[...] = acc


def _outproj(oa, ob, oc, w_out, x, l, tm=512):
    t, d = x.shape
    row = lambda i: (i, 0)
    return pl.pallas_call(
        _outproj_kernel,
        grid=(t // tm,),
        in_specs=[pl.BlockSpec((tm, RWKV_W), row), pl.BlockSpec((tm, RET_W), row),
                  pl.BlockSpec((tm, GDN_W), row),
                  pl.BlockSpec((None, d, d), lambda i: (l, 0, 0)), pl.BlockSpec((tm, d), row)],
        out_specs=pl.BlockSpec((tm, d), row),
        out_shape=jax.ShapeDtypeStruct((t, d), F32),
        compiler_params=_cparams(("parallel",)),
        name="out_proj",
    )(oa, ob, oc, w_out, x)


def _rope_kernel(pos_ref, inv_ref, cos_ref, sin_ref):
    ang = pos_ref[...].astype(F32) * inv_ref[...]
    lane = lax.broadcasted_iota(jnp.int32, ang.shape, 1)
    s = jnp.sin(ang)
    cos_ref[...] = jnp.cos(ang)
    sin_ref[...] = jnp.where(lane < LANES // 2, -s, s)


def _rope_tables(positions, tm=512):
    t = positions.size
    inv = ROPE_BASE ** (-jnp.arange(0, LANES, 2, dtype=F32) / LANES)
    inv = jnp.concatenate([inv, inv]).reshape(1, LANES)
    tm = min(tm, t)
    return pl.pallas_call(
        _rope_kernel,
        grid=(t // tm,),
        in_specs=[pl.BlockSpec((tm, 1), lambda i: (i, 0)), pl.BlockSpec((1, LANES), lambda i: (0, 0))],
        out_specs=[pl.BlockSpec((tm, LANES), lambda i: (i, 0))] * 2,
        out_shape=[jax.ShapeDtypeStruct((t, LANES), F32)] * 2,
        compiler_params=_cparams(("parallel",)),
        name="rope_tables",
    )(positions.reshape(t, 1), inv)


def _rwkv_kernel(*refs, tt, nb, npp, has_vres):
    if has_vres:
        (r_ref, k_ref, v_ref, g_ref, ext_ref, mu_ref, par_ref, w2_ref, a2_ref, v2_ref, vf_ref,
         out_ref, state_ref, carry_ref, ecarry_ref) = refs
    else:
        (r_ref, k_ref, v_ref, g_ref, ext_ref, mu_ref, par_ref, w2_ref, a2_ref,
         out_ref, vf_out_ref, state_ref, carry_ref, ecarry_ref) = refs
    c = CHUNK
    hd = RWKV_HEAD
    nc = tt // c

    @pl.when(pl.program_id(1) == 0)
    def _():
        state_ref[...] = jnp.zeros_like(state_ref)
        carry_ref[...] = jnp.zeros_like(carry_ref)
        ecarry_ref[...] = jnp.zeros_like(ecarry_ref)

    lane = lax.broadcasted_iota(jnp.int32, (1, LANES), 1)
    first = lane < hd

    def head_sum(x):
        s1 = jnp.sum(jnp.where(first, x, 0.0), axis=-1, keepdims=True)
        s2 = jnp.sum(jnp.where(first, 0.0, x), axis=-1, keepdims=True)
        return jnp.where(first, s1, s2)

    def expand(x):
        return jnp.concatenate([jnp.where(first, x, 0.0), jnp.where(first, 0.0, x)], axis=0)

    mu = mu_ref[...]
    par = par_ref[...]

    row = lax.broadcasted_iota(jnp.int32, (c, 2 * c), 0)
    col = lax.broadcasted_iota(jnp.int32, (c, 2 * c), 1) & (c - 1)
    strict = col < row
    incl = col <= row
    eye = jnp.where(col == row, 1.0, 0.0)
    tri_r = lax.broadcasted_iota(jnp.int32, (c, c), 0)
    tri_c = lax.broadcasted_iota(jnp.int32, (c, c), 1)
    tril = jnp.where(tri_c <= tri_r, 1.0, 0.0).astype(BF16)
    srow = lax.broadcasted_iota(jnp.int32, (LANES, LANES), 0)
    scol = lax.broadcasted_iota(jnp.int32, (LANES, LANES), 1)
    same_head = (srow < hd) == (scol < hd)

    rows = {}
    for b in range(nb):
        def mixed(ref, i):
            x = ref[b]
            xs = _shift_rows(carry_ref[b, i], x, 1)
            carry_ref[b, i] = x[tt - 8:]
            return x + (xs - x) * mu[i:i + 1]

        r_all = mixed(r_ref, 0)
        k_all = mixed(k_ref, 1)
        v_all = mixed(v_ref, 2)
        g_all = mixed(g_ref, 3)
        ext = ext_ref[b]
        exts = _shift_rows(ecarry_ref[b], ext, 1)
        ecarry_ref[b] = ext[tt - 8:]

        def low_rank(i):
            return (ext[:, 2 * i * LANES:(2 * i + 1) * LANES]
                    + exts[:, (2 * i + 1) * LANES:(2 * i + 2) * LANES])

        lr_w = jnp.tanh(low_rank(0))
        lr_a = low_rank(1)
        lr_v = low_rank(2) if has_vres else None
        for pi in range(npp):
            ls = slice(pi * LANES, (pi + 1) * LANES)
            w0, a0, v0, k_k, k_a, ln_w, ln_b, r_k = (par[i:i + 1, ls] for i in range(8))
            r, k, v, g = r_all[:, ls], k_all[:, ls], v_all[:, ls], g_all[:, ls]
            log_w = -RWKV_DECAY_SCALE * _sigmoid(w0 + _dot(lr_w, w2_ref[:, ls]))
            a = _sigmoid(a0 + _dot(lr_a, a2_ref[:, ls]))
            if has_vres:
                v = v + (vf_ref[b, :, ls] - v) * _sigmoid(v0 + _dot(lr_v, v2_ref[:, ls]))
            else:
                vf_out_ref[b, :, ls] = v
            kk = k * k_k
            kk = kk * lax.rsqrt(head_sum(kk * kk) + L2_EPS)
            k = k * (1.0 + (a - 1.0) * k_a)
            bonus = head_sum(r * k * r_k) * v
            rows[b, pi] = dict(r=r, k=k, v=v, g=g, log_w=log_w, kk=kk, kka=kk * a, bonus=bonus,
                               ln_w=ln_w, ln_b=ln_b)

    items = [(ci, b, pi) for ci in range(nc) for b in range(nb) for pi in range(npp)]
    pre = {}
    for ci, b, pi in items:
        sl = slice(ci * c, (ci + 1) * c)
        rw = rows[b, pi]
        lw = rw["log_w"][sl]
        gc = _cumsum_rows(lw, tril)
        g_last = gc[c - 1:c]
        dec_inv = jnp.exp(-gc)
        dec_out = jnp.exp(g_last - gc)
        a_t = -rw["kk"][sl] * jnp.exp(gc - lw)
        r_t = rw["r"][sl] * jnp.exp(gc)
        b_t = rw["kka"][sl] * dec_inv
        k_t = rw["k"][sl] * dec_inv
        sc = _dot_nt(jnp.concatenate([a_t, r_t], axis=0),
                     jnp.concatenate([expand(b_t), expand(k_t)], axis=0))
        pre[ci, b, pi] = dict(
            a_t=a_t, r_t=r_t, v=rw["v"][sl],
            l_ab=jnp.where(strict, sc[:c, :2 * c], 0.0),
            l_ak=jnp.where(strict, sc[:c, 2 * c:], 0.0),
            m_rb=jnp.where(incl, sc[c:, :2 * c], 0.0),
            m_rk=jnp.where(incl, sc[c:, 2 * c:], 0.0),
            kd=jnp.concatenate([rw["kka"][sl] * dec_out, rw["k"][sl] * dec_out], axis=0),
            p_end=jnp.exp(g_last))
    t_invs = _unit_lower_inverses([pre[it]["l_ab"] for it in items], eye, expand, 5)
    for it, t_inv in zip(items, t_invs):
        p = pre[it]
        ev = expand(p["v"])
        wu = _dot(t_inv, jnp.concatenate([expand(p["a_t"]), expand(_dot(p["l_ak"], ev))], axis=1))
        p["w_t"] = wu[:, :LANES]
        p["u0"] = wu[:, LANES:]
        p["o0"] = _dot(p["m_rk"], ev)

    keys = [(b, pi) for b in range(nb) for pi in range(npp)]
    states = {key: state_ref[key] for key in keys}
    for ci in range(nc):
        sl = slice(ci * c, (ci + 1) * c)
        for b, pi in keys:
            p = pre[ci, b, pi]
            rw = rows[b, pi]
            state = states[b, pi]
            u = p["u0"] + _dot_nt(p["w_t"], state)
            o = p["o0"] + _dot_nt(p["r_t"], state) + _dot(p["m_rb"], expand(u))
            upd = _dot_tn(jnp.concatenate([u, p["v"]], axis=0), p["kd"])
            states[b, pi] = state * p["p_end"] + jnp.where(same_head, upd, 0.0)

            mean = head_sum(o) * (1.0 / hd)
            cen = o - mean
            var = head_sum(cen * cen) * (1.0 / hd)
            y = cen * lax.rsqrt(var + RWKV_GN_EPS) * rw["ln_w"] + rw["ln_b"]
            out_ref[b, sl, pi * LANES:(pi + 1) * LANES] = (
                (y + rw["bonus"][sl]) * _sigmoid(rw["g"][sl])).astype(out_ref.dtype)
    for key in keys:
        state_ref[key] = states[key]


def _rwkv_mix(pm, pe, mu_rkvg, par, w2, a2, v2, v_first, tt):
    nb, seq, _ = pm.shape
    nt = seq // tt
    has_vres = v_first is not None
    npp = 2
    ngrp = RWKV_PAIRS // npp
    wide = npp * LANES

    def col(off):
        return pl.BlockSpec((nb, tt, wide), lambda p, i: (0, i, off + p))

    pcol = lambda nrows: pl.BlockSpec((nrows, wide), lambda p, i: (0, p))
    in_specs = [col(0), col(ngrp), col(2 * ngrp), col(3 * ngrp),
                pl.BlockSpec((nb, tt, 6 * LANES), lambda p, i: (0, i, 0)),
                pcol(4), pcol(8), pcol(LANES), pcol(LANES)]
    args = [pm, pm, pm, pm, pe, mu_rkvg, par, w2, a2]
    out_block = pl.BlockSpec((nb, tt, wide), lambda p, i: (0, i, p))
    if has_vres:
        in_specs += [pcol(LANES), out_block]
        args += [v2, v_first]
        out_specs = out_block
        out_shape = jax.ShapeDtypeStruct((nb, seq, RWKV_W), BF16)
    else:
        out_specs = [out_block, out_block]
        out_shape = [jax.ShapeDtypeStruct((nb, seq, RWKV_W), BF16),
                     jax.ShapeDtypeStruct((nb, seq, RWKV_W), F32)]
    res = pl.pallas_call(
        functools.partial(_rwkv_kernel, tt=tt, nb=nb, npp=npp, has_vres=has_vres),
        grid=(ngrp, nt),
        in_specs=in_specs,
        out_specs=out_specs,
        out_shape=out_shape,
        scratch_shapes=[pltpu.VMEM((nb, npp, LANES, LANES), F32), pltpu.VMEM((nb, 4, 8, wide), F32),
                        pltpu.VMEM((nb, 8, 6 * LANES), F32)],
        compiler_params=_cparams(("parallel", "arbitrary")),
        name="rwkv7_vres" if has_vres else "rwkv7",
    )(*args)
    if has_vres:
        return res, v_first
    return res[0], res[1]


def _ret_kernel(q_ref, k_ref, v_ref, g_ref, cos_ref, sin_ref, gn_ref, out_ref, state_ref, *, tt, nb):
    c = RET_CHUNK

    @pl.when(pl.program_id(1) == 0)
    def _():
        state_ref[...] = jnp.zeros_like(state_ref)

    head = pl.program_id(0).astype(F32)
    log_gamma = jnp.log(1.0 - jnp.exp2(jnp.full((1, 1), -5.0, F32) - head))
    ri = lax.broadcasted_iota(jnp.int32, (c, c), 0)
    ci_ = lax.broadcasted_iota(jnp.int32, (c, c), 1)
    diff = (ri - ci_).astype(F32)
    dmask = jnp.where(diff >= 0, jnp.exp(jnp.maximum(diff, 0.0) * log_gamma), 0.0)
    idx = lax.broadcasted_iota(jnp.int32, (c, 1), 0).astype(F32)
    k_dec = jnp.exp((c - 1.0 - idx) * log_gamma)
    q_dec = jnp.exp((idx + 1.0) * log_gamma)
    chunk_decay = jnp.exp(c * log_gamma)

    qs, ks = [], []
    for b in range(nb):
        cos = cos_ref[b]
        sin = sin_ref[b]

        def rotary(x):
            return x * cos + pltpu.roll(x, LANES // 2, axis=1) * sin

        qs.append(rotary(q_ref[b]))
        ks.append(rotary(k_ref[b]) * (LANES ** -0.5))

    states = [state_ref[b] for b in range(nb)]
    for i in range(tt // c):
        sl = slice(i * c, (i + 1) * c)
        for b in range(nb):
            qc, kc, vc = qs[b][sl], ks[b][sl], v_ref[b, sl, :]
            scores = _dot_nt(qc, kc) * dmask
            y = _dot(scores, vc) + _dot(qc * q_dec, states[b])
            states[b] = states[b] * chunk_decay + _dot_tn(kc * k_dec, vc)
            mean = jnp.mean(y, axis=-1, keepdims=True)
            cen = y - mean
            var = jnp.mean(cen * cen, axis=-1, keepdims=True)
            yn = cen * lax.rsqrt(var + RET_GN_EPS) * gn_ref[...]
            out_ref[b, sl, :] = (_silu(g_ref[b, sl, :]) * yn).astype(out_ref.dtype)
    for b in range(nb):
        state_ref[b] = states[b]


def _ret_mix(pm, cos_t, sin_t, gn_w, tt):
    nb, seq, _ = pm.shape
    nt = seq // tt
    base = 4 * RWKV_PAIRS

    def col(off):
        return pl.BlockSpec((nb, tt, LANES), lambda h, i: (0, i, base + off + h))

    tab = pl.BlockSpec((nb, tt, LANES), lambda h, i: (0, i, 0))
    return pl.pallas_call(
        functools.partial(_ret_kernel, tt=tt, nb=nb),
        grid=(RET_HEADS, nt),
        in_specs=[col(0), col(RET_HEADS), col(2 * RET_HEADS), col(3 * RET_HEADS), tab, tab,
                  pl.BlockSpec((1, LANES), lambda h, i: (0, h))],
        out_specs=pl.BlockSpec((nb, tt, LANES), lambda h, i: (0, i, h)),
        out_shape=jax.ShapeDtypeStruct((nb, seq, RET_W), BF16),
        scratch_shapes=[pltpu.VMEM((nb, LANES, LANES), F32)],
        compiler_params=_cparams(("parallel", "arbitrary")),
        name="retention",
    )(pm, pm, pm, pm, cos_t, sin_t, gn_w.reshape(1, RET_W))


def _gdn_kernel(q_ref, k_ref, v_ref, z_ref, ab_ref, cw_ref, par_ref, nw_ref,
                out_ref, state_ref, carry_ref, *, tt, nb, nh):
    c = CHUNK
    nc = tt // c

    @pl.when(pl.program_id(1) == 0)
    def _():
        state_ref[...] = jnp.zeros_like(state_ref)
        carry_ref[...] = jnp.zeros_like(carry_ref)

    cw = cw_ref[...]

    def softplus(x):
        return jnp.maximum(x, 0.0) + jnp.log(1.0 + jnp.exp(-jnp.abs(x)))

    def l2norm(x):
        return x * lax.rsqrt(jnp.sum(x * x, axis=-1, keepdims=True) + L2_EPS)

    ri = lax.broadcasted_iota(jnp.int32, (c, c), 0)
    ci_ = lax.broadcasted_iota(jnp.int32, (c, c), 1)
    causal = ci_ <= ri
    strict = ci_ < ri
    eye = jnp.where(ci_ == ri, 1.0, 0.0)
    lane = lax.broadcasted_iota(jnp.int32, (1, LANES), 1)
    sub = lax.broadcasted_iota(jnp.int32, (2 * 8, 1), 0)

    rows = {}
    for b in range(nb):
        def conv_silu(ref, i):
            x = ref[b]
            prev = carry_ref[b, i]
            acc = x * cw[i, GDN_CONV - 1:GDN_CONV]
            for j in range(GDN_CONV - 1):
                acc = acc + _shift_rows(prev, x, GDN_CONV - 1 - j) * cw[i, j:j + 1]
            carry_ref[b, i] = x[tt - 8:]
            return _silu(acc)

        q_all = conv_silu(q_ref, 0)
        k_all = conv_silu(k_ref, 1)
        v_all = conv_silu(v_ref, 2)
        ab = ab_ref[b]
        ab_t = ab.T[:2 * 8]
        for hi in range(nh):
            ls = slice(hi * LANES, (hi + 1) * LANES)
            head = pl.program_id(0) * nh + hi
            par = par_ref[hi]
            neg_rate = -jnp.exp(par[:, 0:1])
            dt_bias = par[:, 1:2]
            a_col = jnp.sum(jnp.where(lane == head, ab, 0.0), axis=-1, keepdims=True)
            b_col = jnp.sum(jnp.where(lane == head + GDN_HEADS, ab, 0.0), axis=-1, keepdims=True)
            a_row = jnp.sum(jnp.where(sub == head, ab_t, 0.0), axis=0, keepdims=True)
            rows[b, hi] = dict(q=l2norm(q_all[:, ls]) * (LANES ** -0.5), k=l2norm(k_all[:, ls]),
                               v=v_all[:, ls],
                               g_col=neg_rate * softplus(a_col + dt_bias),
                               g_row=neg_rate * softplus(a_row + dt_bias),
                               beta=_sigmoid(b_col))

    keys = [(b, hi) for b in range(nb) for hi in range(nh)]
    items = [(ci,) + key for ci in range(nc) for key in keys]
    pre = {}
    for ci, b, hi in items:
        sl = slice(ci * c, (ci + 1) * c)
        rw = rows[b, hi]
        gr = rw["g_row"][:, sl]
        gcl = rw["g_col"][sl]
        gc_col = jnp.sum(jnp.where(causal, gr, 0.0), axis=1, keepdims=True)
        gc_row = jnp.sum(jnp.where(ri <= ci_, gcl, 0.0), axis=0, keepdims=True)
        g_last = gc_col[c - 1:c]
        decay = jnp.where(causal, jnp.exp(jnp.minimum(gc_col - gc_row, 0.0)), 0.0)
        qc, kc, vc, bc = rw["q"][sl], rw["k"][sl], rw["v"][sl], rw["beta"][sl]
        kb = kc * bc
        kk_t = _dot_nt(jnp.concatenate([kb, qc], axis=0), kc)
        e_gc = jnp.exp(gc_col)
        pre[ci, b, hi] = dict(
            lower=jnp.where(strict, kk_t[:c] * decay, 0.0),
            attn=kk_t[c:] * decay,
            vk=jnp.concatenate([vc * bc, kb * e_gc], axis=1),
            q_in=qc * e_gc,
            k_out=kc * jnp.exp(g_last - gc_col),
            p_end=jnp.exp(g_last))
    t_invs = _unit_lower_inverses([-pre[it]["lower"] for it in items], eye, lambda x: x, 5)
    for it, t_inv in zip(items, t_invs):
        pre[it]["uw"] = _dot(t_inv, pre[it]["vk"])

    states = {key: state_ref[key] for key in keys}
    for ci in range(nc):
        sl = slice(ci * c, (ci + 1) * c)
        for b, hi in keys:
            ls = slice(hi * LANES, (hi + 1) * LANES)
            p = pre[ci, b, hi]
            state = states[b, hi]
            v_new = p["uw"][:, :LANES] - _dot(p["uw"][:, LANES:], state)
            o = _dot(p["q_in"], state) + _dot(p["attn"], v_new)
            states[b, hi] = state * p["p_end"] + _dot_tn(p["k_out"], v_new)
            on = o * lax.rsqrt(jnp.mean(o * o, axis=-1, keepdims=True) + NORM_EPS)
            out_ref[b, sl, ls] = (on * nw_ref[:, ls] * _silu(z_ref[b, sl, ls])).astype(out_ref.dtype)
    for key in keys:
        state_ref[key] = states[key]


def _gdn_mix(pm, pe, conv_w, par, norm_w, tt):
    nb, seq, _ = pm.shape
    nt = seq // tt
    nh = 1
    ngrp = GDN_HEADS // nh
    wide = nh * LANES
    base = (4 * RWKV_PAIRS + 4 * RET_HEADS) // nh

    def col(off):
        return pl.BlockSpec((nb, tt, wide), lambda h, i: (0, i, base + off + h))

    return pl.pallas_call(
        functools.partial(_gdn_kernel, tt=tt, nb=nb, nh=nh),
        grid=(ngrp, nt),
        in_specs=[col(0), col(ngrp), col(2 * ngrp), col(3 * ngrp),
                  pl.BlockSpec((nb, tt, LANES), lambda h, i: (0, i, 6)),
                  pl.BlockSpec((3, GDN_CONV, wide), lambda h, i: (0, 0, h)),
                  pl.BlockSpec((nh, 1, LANES), lambda h, i: (h, 0, 0)),
                  pl.BlockSpec((1, wide), lambda h, i: (0, h))],
        out_specs=pl.BlockSpec((nb, tt, wide), lambda h, i: (0, i, h)),
        out_shape=jax.ShapeDtypeStruct((nb, seq, GDN_W), BF16),
        scratch_shapes=[pltpu.VMEM((nb, nh, LANES, LANES), F32), pltpu.VMEM((nb, 3, 8, wide), F32)],
        compiler_params=_cparams(("parallel", "arbitrary")),
        name="gated_deltanet",
    )(pm, pm, pm, pm, pe, conv_w, par, norm_w.reshape(1, GDN_W))


def _router_kernel(x_ref, nw_ref, wr_ref, br_ref, h_ref, gates_ref):
    x = x_ref[...]
    h = x * lax.rsqrt(jnp.mean(x * x, axis=-1, keepdims=True) + NORM_EPS) * nw_ref[...]
    h_hi = h.astype(BF16)
    h_ref[...] = h_hi
    h_lo = (h - h_hi.astype(F32)).astype(BF16)
    wr = wr_ref[...]
    w_hi = wr.astype(BF16)
    w_lo = (wr - w_hi.astype(F32)).astype(BF16)
    logits = (jnp.dot(h_hi, w_hi, preferred_element_type=F32)
              + (jnp.dot(h_hi, w_lo, preferred_element_type=F32)
                 + jnp.dot(h_lo, w_hi, preferred_element_type=F32))) + br_ref[...]
    lane_i = lax.broadcasted_iota(jnp.int32, logits.shape, 1)
    lane = lane_i.astype(F32)
    group_of_lane = (lane_i // MOE_PER_GROUP).astype(F32)
    neg = -jnp.inf
    is_group = (lane_i >= MOE_EXPERTS) & (lane_i < MOE_EXPERTS + MOE_GROUPS)
    gl = jnp.where(is_group, logits, neg)
    gmax = jnp.max(gl, axis=-1, keepdims=True)
    gidx = jnp.min(jnp.where(gl == gmax, lane - MOE_EXPERTS, LANES), axis=-1, keepdims=True)
    group_gate = 1.0 / jnp.sum(jnp.exp(gl - gmax), axis=-1, keepdims=True)
    in_group = (lane_i < MOE_EXPERTS) & (group_of_lane == gidx)
    el = jnp.where(in_group, logits, neg)
    m1 = jnp.max(el, axis=-1, keepdims=True)
    i1 = jnp.min(jnp.where(el == m1, lane, LANES), axis=-1, keepdims=True)
    el2 = jnp.where(lane == i1, neg, el)
    m2 = jnp.max(el2, axis=-1, keepdims=True)
    i2 = jnp.min(jnp.where(el2 == m2, lane, LANES), axis=-1, keepdims=True)
    esum = jnp.sum(jnp.exp(el - m1), axis=-1, keepdims=True)
    p1 = 1.0 / esum
    p2 = jnp.exp(m2 - m1) / esum
    tot = p1 + p2
    within = jnp.where(lane == i1, p1 / tot, jnp.where(lane == i2, p2 / tot, 0.0))
    gates_ref[...] = group_gate * within


def _router(x, norm_w, wr, br, tm=512):
    t, d = x.shape
    row = lambda i: (i, 0)
    const = lambda i: (0, 0)
    return pl.pallas_call(
        _router_kernel,
        grid=(t // tm,),
        in_specs=[pl.BlockSpec((tm, d), row), pl.BlockSpec((1, d), const),
                  pl.BlockSpec((d, LANES), const), pl.BlockSpec((1, LANES), const)],
        out_specs=[pl.BlockSpec((tm, d), row), pl.BlockSpec((tm, LANES), row)],
        out_shape=[jax.ShapeDtypeStruct((t, d), BF16), jax.ShapeDtypeStruct((t, LANES), F32)],
        compiler_params=_cparams(("parallel",)),
        name="moe_router",
    )(x, norm_w.reshape(1, d), wr, br)


def _moe_dense_kernel(h_ref, g_ref, w1_ref, w3_ref, w2_ref, x_ref, o_ref, *, te):
    j = pl.program_id(1)

    @pl.when(j == 0)
    def _():
        o_ref[...] = x_ref[...]

    h = h_ref[...]
    gts = g_ref[...]
    lane = lax.broadcasted_iota(jnp.int32, (1, LANES), 1)
    acc = o_ref[...]
    for e in range(te):
        gate = jnp.sum(jnp.where(lane == j * te + e, gts, 0.0), axis=-1, keepdims=True)
        a1 = jnp.dot(h, w1_ref[e], preferred_element_type=F32)
        a3 = jnp.dot(h, w3_ref[e], preferred_element_type=F32)
        hid = _silu(a1) * a3 * gate
        acc = acc + jnp.dot(hid.astype(BF16), w2_ref[e], preferred_element_type=F32)
    o_ref[...] = acc


def _moe_dense(h, gates, w1, w3, w2, x, l, tm=512, te=4):
    t, d = x.shape
    ne = MOE_EXPERTS // te
    return pl.pallas_call(
        functools.partial(_moe_dense_kernel, te=te),
        grid=(t // tm, ne),
        in_specs=[pl.BlockSpec((tm, d), lambda i, j: (i, 0)),
                  pl.BlockSpec((tm, LANES), lambda i, j: (i, 0)),
                  pl.BlockSpec((None, te, d, MOE_HIDDEN), lambda i, j: (l, j, 0, 0)),
                  pl.BlockSpec((None, te, d, MOE_HIDDEN), lambda i, j: (l, j, 0, 0)),
                  pl.BlockSpec((None, te, MOE_HIDDEN, d), lambda i, j: (l, j, 0, 0)),
                  pl.BlockSpec((tm, d), lambda i, j: (i, 0))],
        out_specs=pl.BlockSpec((tm, d), lambda i, j: (i, 0)),
        out_shape=jax.ShapeDtypeStruct((t, d), F32),
        compiler_params=_cparams(("parallel", "arbitrary")),
        name="moe_experts",
    )(h, gates, w1, w3, w2, x)


def _pad_cols(w, n):
    return jnp.pad(w, ((0, 0), (0, n - w.shape[1])))


def _pad_rows(w, n):
    return jnp.pad(w, ((0, n - w.shape[0]), (0, 0)))


def _mixers(l, h, batch, tt, cos_t, sin_t, v_first, w_in, rwkv_mu_rkvg, rwkv_mu_wa, rwkv_w0, rwkv_w1,
            rwkv_w2, rwkv_a0, rwkv_a1, rwkv_a2, rwkv_k_k, rwkv_k_a, rwkv_r_k, rwkv_ln_w, rwkv_ln_b,
            rwkv_mu_vres, rwkv_v0, rwkv_v1, rwkv_v2, ret_gn_w, gdn_conv_w, gdn_a_log, gdn_dt_bias,
            gdn_norm_w):
    t = h.shape[0]
    seq = t // batch
    d = D_MODEL
    wt_in = w_in[l].T.astype(BF16)
    mu_w = rwkv_mu_wa[l][0][None, :]
    mu_a = rwkv_mu_wa[l][1][None, :]
    w1t, a1t = rwkv_w1[l].T, rwkv_a1[l].T
    blocks = [(1.0 - mu_w) * w1t, mu_w * w1t, (1.0 - mu_a) * a1t, mu_a * a1t]
    if l > 0:
        mu_v = rwkv_mu_vres[l - 1][None, :]
        v1t = rwkv_v1[l - 1].T
        blocks += [(1.0 - mu_v) * v1t, mu_v * v1t]
        v0 = rwkv_v0[l - 1]
        v2 = _pad_rows(rwkv_v2[l - 1], LANES).astype(BF16)
    else:
        blocks += [jnp.zeros((LANES, d), F32)] * 2
        v0 = jnp.zeros((RWKV_W,), F32)
        v2 = None
    blocks.append(wt_in[N_MAIN:])
    wt_ext = jnp.concatenate([_pad_rows(b.astype(BF16), LANES) for b in blocks], axis=0)

    pm = _matmul_nt(h, wt_in, N_MAIN, tm=min(1024, t), tn=1024).reshape(batch, seq, N_MAIN)
    pe = _matmul_nt(h, wt_ext, N_EXT, tm=min(1024, t), tn=N_EXT).reshape(batch, seq, N_EXT)

    par = jnp.stack([rwkv_w0[l], rwkv_a0[l], v0, rwkv_k_k[l], rwkv_k_a[l], rwkv_ln_w[l], rwkv_ln_b[l],
                     rwkv_r_k[l].reshape(RWKV_W)])
    out_a, v_first = _rwkv_mix(pm, pe, rwkv_mu_rkvg[l], par,
                               _pad_rows(rwkv_w2[l], LANES).astype(BF16),
                               _pad_rows(rwkv_a2[l], LANES).astype(BF16), v2, v_first, tt)
    out_b = _ret_mix(pm, cos_t, sin_t, ret_gn_w[l], tt)
    gpar = _pad_cols(jnp.stack([gdn_a_log[l], gdn_dt_bias[l]], axis=1), LANES).reshape(GDN_HEADS, 1, LANES)
    out_c = _gdn_mix(pm, pe, gdn_conv_w[l].reshape(GDN_CONV, 3, GDN_W).transpose(1, 0, 2), gpar,
                     gdn_norm_w[l], tt)
    return (out_a.reshape(t, RWKV_W), out_b.reshape(t, RET_W), out_c.reshape(t, GDN_W), v_first)


def kernel(x, positions, norm1_w, w_in, w_out, rwkv_mu_rkvg, rwkv_mu_wa, rwkv_w0, rwkv_w1, rwkv_w2, rwkv_a0, rwkv_a1, rwkv_a2, rwkv_k_k, rwkv_k_a, rwkv_r_k, rwkv_ln_w, rwkv_ln_b, rwkv_mu_vres, rwkv_v0, rwkv_v1, rwkv_v2, ret_gn_w, gdn_conv_w, gdn_a_log, gdn_dt_bias, gdn_norm_w, norm2_w, moe_group_w, moe_group_b, moe_expert_w, moe_expert_b, moe_w1, moe_w3, moe_w2, final_norm_w):
    batch, seq, d = x.shape
    t = batch * seq
    depth = w_in.shape[0]
    tt = min(256, seq)
    tm = min(512, t)
    xf = x.reshape(t, d)
    cos_t, sin_t = _rope_tables(positions, tm)
    cos_t = cos_t.reshape(batch, seq, LANES)
    sin_t = sin_t.reshape(batch, seq, LANES)
    w1_bf, w3_bf, w2_bf = moe_w1.astype(BF16), moe_w3.astype(BF16), moe_w2.astype(BF16)
    w_out_bf = w_out.astype(BF16)
    v_first = None
    for l in range(depth):
        h = _rmsnorm(xf, norm1_w[l], BF16, tm)
        out_a, out_b, out_c, v_first = _mixers(
            l, h, batch, tt, cos_t, sin_t, v_first, w_in, rwkv_mu_rkvg, rwkv_mu_wa, rwkv_w0, rwkv_w1,
            rwkv_w2, rwkv_a0, rwkv_a1, rwkv_a2, rwkv_k_k, rwkv_k_a, rwkv_r_k, rwkv_ln_w, rwkv_ln_b,
            rwkv_mu_vres, rwkv_v0, rwkv_v1, rwkv_v2, ret_gn_w, gdn_conv_w, gdn_a_log, gdn_dt_bias,
            gdn_norm_w)
        xf = _outproj(out_a, out_b, out_c, w_out_bf, xf, l, tm)
        wr = _pad_cols(jnp.concatenate([moe_expert_w[l], moe_group_w[l]], axis=1), LANES)
        br = _pad_cols(jnp.concatenate([moe_expert_b[l], moe_group_b[l]]).reshape(1, -1), LANES)
        h2, gates = _router(xf, norm2_w[l], wr, br, tm)
        xf = _moe_dense(h2, gates, w1_bf, w3_bf, w2_bf, xf, l, tm)
    out = _rmsnorm(xf, final_norm_w, F32, tm)
    return out.reshape(batch, seq, d)
```

```python
import functools

import jax
import jax.numpy as jnp
from jax import lax
from jax.experimental import pallas as pl
from jax.experimental.pallas import tpu as pltpu

F32 = jnp.float32
BF16 = jnp.bfloat16

D_MODEL = 2048
RWKV_W = 768
RET_W = 512
GDN_W = 768
RWKV_HEAD = 64
RWKV_PAIRS = RWKV_W // 128
RET_HEADS = RET_W // 128
GDN_HEADS = GDN_W // 128
LANES = 128
RWKV_DECAY_SCALE = 0.6065306597126334
RWKV_GN_EPS = 64e-5
RET_GN_EPS = 1e-5
ROPE_BASE = 10000.0
GDN_CONV = 4
MOE_GROUPS = 4
MOE_PER_GROUP = 8
MOE_EXPERTS = 32
MOE_HIDDEN = 256
NORM_EPS = 1e-6
L2_EPS = 1e-6
N_MAIN = 4 * RWKV_W + 4 * RET_W + 4 * GDN_W
N_EXT = 7 * LANES
CHUNK = 64
RET_CHUNK = 128
VMEM_LIMIT = 56 * 1024 * 1024


def _cparams(sem):
    return pltpu.CompilerParams(dimension_semantics=sem, vmem_limit_bytes=VMEM_LIMIT)


def _dot(a, b):
    return jnp.dot(a.astype(BF16), b.astype(BF16), preferred_element_type=F32)


def _dot_nt(a, b):
    return lax.dot_general(a.astype(BF16), b.astype(BF16), (((1,), (1,)), ((), ())),
                           preferred_element_type=F32)


def _dot_tn(a, b):
    return lax.dot_general(a.astype(BF16), b.astype(BF16), (((0,), (0,)), ((), ())),
                           preferred_element_type=F32)


def _sigmoid(x):
    return 1.0 / (1.0 + jnp.exp(-x))


def _silu(x):
    return x * _sigmoid(x)


def _shift_rows(prev8, x, n):
    xs = jnp.concatenate([prev8, x], axis=0)
    return pltpu.roll(xs, n, axis=0)[8:]


def _cumsum_rows(x, tril):
    hi = x.astype(BF16)
    lo = (x - hi.astype(F32)).astype(BF16)
    return (jnp.dot(tril, hi, preferred_element_type=F32)
            + jnp.dot(tril, lo, preferred_element_type=F32))


def _unit_lower_inverses(lows, eye, expand, steps):
    xs = [eye + low for low in lows]
    ps = list(lows)
    for _ in range(steps):
        ps = [_dot(p, expand(p)) for p in ps]
        xs = [x + _dot(x, expand(p)) for x, p in zip(xs, ps)]
    return xs


def _rmsnorm_kernel(x_ref, w_ref, o_ref):
    x = x_ref[...]
    y = x * lax.rsqrt(jnp.mean(x * x, axis=-1, keepdims=True) + NORM_EPS) * w_ref[...]
    o_ref[...] = y.astype(o_ref.dtype)


def _rmsnorm(x, w, out_dtype, tm=512):
    t, d = x.shape
    return pl.pallas_call(
        _rmsnorm_kernel,
        grid=(t // tm,),
        in_specs=[pl.BlockSpec((tm, d), lambda i: (i, 0)), pl.BlockSpec((1, d), lambda i: (0, 0))],
        out_specs=pl.BlockSpec((tm, d), lambda i: (i, 0)),
        out_shape=jax.ShapeDtypeStruct((t, d), out_dtype),
        compiler_params=_cparams(("parallel",)),
        name="rmsnorm",
    )(x, w.reshape(1, d))


def _matmul_nt_kernel(a_ref, bt_ref, o_ref):
    o_ref[...] = lax.dot_general(a_ref[...], bt_ref[...], (((1,), (1,)), ((), ())),
                                 preferred_element_type=F32).astype(o_ref.dtype)


def _matmul_nt(a, bt, n, tm, tn, l=None, out_dtype=F32):
    m, k = a.shape
    if l is None:
        w_spec = pl.BlockSpec((tn, k), lambda j, i: (j, 0))
    else:
        w_spec = pl.BlockSpec((None, tn, k), lambda j, i: (l, j, 0))
    return pl.pallas_call(
        _matmul_nt_kernel,
        grid=(n // tn, m // tm),
        in_specs=[pl.BlockSpec((tm, k), lambda j, i: (i, 0)), w_spec],
        out_specs=pl.BlockSpec((tm, tn), lambda j, i: (i, j)),
        out_shape=jax.ShapeDtypeStruct((m, n), out_dtype),
        compiler_params=_cparams(("parallel", "parallel")),
        name="in_proj",
    )(a, bt)


def _outproj_kernel(a_ref, b_ref, c_ref, w_ref, x_ref, nw_ref, wr_ref, br_ref, o_ref, h_ref, gates_ref):
    acc = x_ref[...]
    acc = acc + jnp.dot(a_ref[...], w_ref[:RWKV_W], preferred_element_type=F32)
    acc = acc + jnp.dot(b_ref[...], w_ref[RWKV_W:RWKV_W + RET_W], preferred_element_type=F32)
    acc = acc + jnp.dot(c_ref[...], w_ref[RWKV_W + RET_W:], preferred_element_type=F32)
    o_ref[...] = acc
    h_ref[...], gates_ref[...] = _route(acc, nw_ref[...], wr_ref[...], br_ref[...])


def _outproj_route(oa, ob, oc, w_out, x, norm_w, wr, br, l, tm=512):
    t, d = x.shape
    row = lambda i: (i, 0)
    const = lambda i: (0, 0)
    return pl.pallas_call(
        _outproj_kernel,
        grid=(t // tm,),
        in_specs=[pl.BlockSpec((tm, RWKV_W), row), pl.BlockSpec((tm, RET_W), row),
                  pl.BlockSpec((tm, GDN_W), row),
                  pl.BlockSpec((None, d, d), lambda i: (l, 0, 0)), pl.BlockSpec((tm, d), row),
                  pl.BlockSpec((1, d), const), pl.BlockSpec((d, LANES), const),
                  pl.BlockSpec((1, LANES), const)],
        out_specs=[pl.BlockSpec((tm, d), row), pl.BlockSpec((tm, d), row),
                   pl.BlockSpec((tm, LANES), row)],
        out_shape=[jax.ShapeDtypeStruct((t, d), F32), jax.ShapeDtypeStruct((t, d), BF16),
                   jax.ShapeDtypeStruct((t, LANES), F32)],
        compiler_params=_cparams(("parallel",)),
        name="out_proj_route",
    )(oa, ob, oc, w_out, x, norm_w.reshape(1, d), wr, br)


def _rope_kernel(pos_ref, inv_ref, cos_ref, sin_ref):
    ang = pos_ref[...].astype(F32) * inv_ref[...]
    lane = lax.broadcasted_iota(jnp.int32, ang.shape, 1)
    s = jnp.sin(ang)
    cos_ref[...] = jnp.cos(ang)
    sin_ref[...] = jnp.where(lane < LANES // 2, -s, s)


def _rope_tables(positions, tm=512):
    t = positions.size
    inv = ROPE_BASE ** (-jnp.arange(0, LANES, 2, dtype=F32) / LANES)
    inv = jnp.concatenate([inv, inv]).reshape(1, LANES)
    tm = min(tm, t)
    return pl.pallas_call(
        _rope_kernel,
        grid=(t // tm,),
        in_specs=[pl.BlockSpec((tm, 1), lambda i: (i, 0)), pl.BlockSpec((1, LANES), lambda i: (0, 0))],
        out_specs=[pl.BlockSpec((tm, LANES), lambda i: (i, 0))] * 2,
        out_shape=[jax.ShapeDtypeStruct((t, LANES), F32)] * 2,
        compiler_params=_cparams(("parallel",)),
        name="rope_tables",
    )(positions.reshape(t, 1), inv)


def _rwkv_kernel(*refs, tt, nb, npp, has_vres):
    if has_vres:
        (r_ref, k_ref, v_ref, g_ref, ext_ref, mu_ref, par_ref, w2_ref, a2_ref, v2_ref, vf_ref,
         out_ref, state_ref, carry_ref, ecarry_ref) = refs
    else:
        (r_ref, k_ref, v_ref, g_ref, ext_ref, mu_ref, par_ref, w2_ref, a2_ref,
         out_ref, vf_out_ref, state_ref, carry_ref, ecarry_ref) = refs
    c = CHUNK
    hd = RWKV_HEAD
    nc = tt // c

    @pl.when(pl.program_id(1) == 0)
    def _():
        state_ref[...] = jnp.zeros_like(state_ref)
        carry_ref[...] = jnp.zeros_like(carry_ref)
        ecarry_ref[...] = jnp.zeros_like(ecarry_ref)

    lane = lax.broadcasted_iota(jnp.int32, (1, LANES), 1)
    first = lane < hd

    def head_sum(x):
        s1 = jnp.sum(jnp.where(first, x, 0.0), axis=-1, keepdims=True)
        s2 = jnp.sum(jnp.where(first, 0.0, x), axis=-1, keepdims=True)
        return jnp.where(first, s1, s2)

    def expand(x):
        return jnp.concatenate([jnp.where(first, x, 0.0), jnp.where(first, 0.0, x)], axis=0)

    mu = mu_ref[...]
    par = par_ref[...]

    row = lax.broadcasted_iota(jnp.int32, (c, 2 * c), 0)
    col = lax.broadcasted_iota(jnp.int32, (c, 2 * c), 1) & (c - 1)
    strict = col < row
    incl = col <= row
    eye = jnp.where(col == row, 1.0, 0.0)
    tri_r = lax.broadcasted_iota(jnp.int32, (c, c), 0)
    tri_c = lax.broadcasted_iota(jnp.int32, (c, c), 1)
    tril = jnp.where(tri_c <= tri_r, 1.0, 0.0).astype(BF16)
    srow = lax.broadcasted_iota(jnp.int32, (LANES, LANES), 0)
    scol = lax.broadcasted_iota(jnp.int32, (LANES, LANES), 1)
    same_head = (srow < hd) == (scol < hd)

    rows = {}
    for b in range(nb):
        def mixed(ref, i):
            x = ref[b]
            xs = _shift_rows(carry_ref[b, i], x, 1)
            carry_ref[b, i] = x[tt - 8:]
            return x + (xs - x) * mu[i:i + 1]

        r_all = mixed(r_ref, 0)
        k_all = mixed(k_ref, 1)
        v_all = mixed(v_ref, 2)
        g_all = mixed(g_ref, 3)
        ext = ext_ref[b]
        exts = _shift_rows(ecarry_ref[b], ext, 1)
        ecarry_ref[b] = ext[tt - 8:]

        def low_rank(i):
            return (ext[:, 2 * i * LANES:(2 * i + 1) * LANES]
                    + exts[:, (2 * i + 1) * LANES:(2 * i + 2) * LANES])

        lr_w = jnp.tanh(low_rank(0))
        lr_a = low_rank(1)
        lr_v = low_rank(2) if has_vres else None
        for pi in range(npp):
            ls = slice(pi * LANES, (pi + 1) * LANES)
            w0, a0, v0, k_k, k_a, ln_w, ln_b, r_k = (par[i:i + 1, ls] for i in range(8))
            r, k, v, g = r_all[:, ls], k_all[:, ls], v_all[:, ls], g_all[:, ls]
            log_w = -RWKV_DECAY_SCALE * _sigmoid(w0 + _dot(lr_w, w2_ref[:, ls]))
            a = _sigmoid(a0 + _dot(lr_a, a2_ref[:, ls]))
            if has_vres:
                v = v + (vf_ref[b, :, ls] - v) * _sigmoid(v0 + _dot(lr_v, v2_ref[:, ls]))
            else:
                vf_out_ref[b, :, ls] = v
            kk = k * k_k
            kk = kk * lax.rsqrt(head_sum(kk * kk) + L2_EPS)
            k = k * (1.0 + (a - 1.0) * k_a)
            bonus = head_sum(r * k * r_k) * v
            rows[b, pi] = dict(r=r, k=k, v=v, g=g, log_w=log_w, kk=kk, kka=kk * a, bonus=bonus,
                               ln_w=ln_w, ln_b=ln_b)

    items = [(ci, b, pi) for ci in range(nc) for b in range(nb) for pi in range(npp)]
    pre = {}
    for ci, b, pi in items:
        sl = slice(ci * c, (ci + 1) * c)
        rw = rows[b, pi]
        lw = rw["log_w"][sl]
        gc = _cumsum_rows(lw, tril)
        g_last = gc[c - 1:c]
        dec_inv = jnp.exp(-gc)
        dec_out = jnp.exp(g_last - gc)
        a_t = -rw["kk"][sl] * jnp.exp(gc - lw)
        r_t = rw["r"][sl] * jnp.exp(gc)
        b_t = rw["kka"][sl] * dec_inv
        k_t = rw["k"][sl] * dec_inv
        sc = _dot_nt(jnp.concatenate([a_t, r_t], axis=0),
                     jnp.concatenate([expand(b_t), expand(k_t)], axis=0))
        pre[ci, b, pi] = dict(
            a_t=a_t, r_t=r_t, v=rw["v"][sl],
            l_ab=jnp.where(strict, sc[:c, :2 * c], 0.0),
            l_ak=jnp.where(strict, sc[:c, 2 * c:], 0.0),
            m_rb=jnp.where(incl, sc[c:, :2 * c], 0.0),
            m_rk=jnp.where(incl, sc[c:, 2 * c:], 0.0),
            kd=jnp.concatenate([rw["kka"][sl] * dec_out, rw["k"][sl] * dec_out], axis=0),
            p_end=jnp.exp(g_last))
    t_invs = _unit_lower_inverses([pre[it]["l_ab"] for it in items], eye, expand, 5)
    for it, t_inv in zip(items, t_invs):
        p = pre[it]
        ev = expand(p["v"])
        wu = _dot(t_inv, jnp.concatenate([expand(p["a_t"]), expand(_dot(p["l_ak"], ev))], axis=1))
        p["w_t"] = wu[:, :LANES]
        p["u0"] = wu[:, LANES:]
        p["o0"] = _dot(p["m_rk"], ev)

    keys = [(b, pi) for b in range(nb) for pi in range(npp)]
    states = {key: state_ref[key] for key in keys}
    for ci in range(nc):
        sl = slice(ci * c, (ci + 1) * c)
        for b, pi in keys:
            p = pre[ci, b, pi]
            rw = rows[b, pi]
            state = states[b, pi]
            u = p["u0"] + _dot_nt(p["w_t"], state)
            o = p["o0"] + _dot_nt(p["r_t"], state) + _dot(p["m_rb"], expand(u))
            upd = _dot_tn(jnp.concatenate([u, p["v"]], axis=0), p["kd"])
            states[b, pi] = state * p["p_end"] + jnp.where(same_head, upd, 0.0)

            mean = head_sum(o) * (1.0 / hd)
            cen = o - mean
            var = head_sum(cen * cen) * (1.0 / hd)
            y = cen * lax.rsqrt(var + RWKV_GN_EPS) * rw["ln_w"] + rw["ln_b"]
            out_ref[b, sl, pi * LANES:(pi + 1) * LANES] = (
                (y + rw["bonus"][sl]) * _sigmoid(rw["g"][sl])).astype(out_ref.dtype)
    for key in keys:
        state_ref[key] = states[key]


def _rwkv_mix(pm, pe, mu_rkvg, par, w2, a2, v2, v_first, tt):
    nb, seq, _ = pm.shape
    nt = seq // tt
    has_vres = v_first is not None
    npp = 2
    ngrp = RWKV_PAIRS // npp
    wide = npp * LANES

    def col(off):
        return pl.BlockSpec((nb, tt, wide), lambda p, i: (0, i, off + p))

    pcol = lambda nrows: pl.BlockSpec((nrows, wide), lambda p, i: (0, p))
    in_specs = [col(0), col(ngrp), col(2 * ngrp), col(3 * ngrp),
                pl.BlockSpec((nb, tt, 6 * LANES), lambda p, i: (0, i, 0)),
                pcol(4), pcol(8), pcol(LANES), pcol(LANES)]
    args = [pm, pm, pm, pm, pe, mu_rkvg, par, w2, a2]
    out_block = pl.BlockSpec((nb, tt, wide), lambda p, i: (0, i, p))
    if has_vres:
        in_specs += [pcol(LANES), out_block]
        args += [v2, v_first]
        out_specs = out_block
        out_shape = jax.ShapeDtypeStruct((nb, seq, RWKV_W), BF16)
    else:
        out_specs = [out_block, out_block]
        out_shape = [jax.ShapeDtypeStruct((nb, seq, RWKV_W), BF16),
                     jax.ShapeDtypeStruct((nb, seq, RWKV_W), F32)]
    res = pl.pallas_call(
        functools.partial(_rwkv_kernel, tt=tt, nb=nb, npp=npp, has_vres=has_vres),
        grid=(ngrp, nt),
        in_specs=in_specs,
        out_specs=out_specs,
        out_shape=out_shape,
        scratch_shapes=[pltpu.VMEM((nb, npp, LANES, LANES), F32), pltpu.VMEM((nb, 4, 8, wide), F32),
                        pltpu.VMEM((nb, 8, 6 * LANES), F32)],
        compiler_params=_cparams(("parallel", "arbitrary")),
        name="rwkv7_vres" if has_vres else "rwkv7",
    )(*args)
    if has_vres:
        return res, v_first
    return res[0], res[1]


def _ret_kernel(q_ref, k_ref, v_ref, g_ref, cos_ref, sin_ref, gn_ref, out_ref, state_ref, *, tt, nb):
    c = RET_CHUNK

    @pl.when(pl.program_id(1) == 0)
    def _():
        state_ref[...] = jnp.zeros_like(state_ref)

    head = pl.program_id(0).astype(F32)
    log_gamma = jnp.log(1.0 - jnp.exp2(jnp.full((1, 1), -5.0, F32) - head))
    ri = lax.broadcasted_iota(jnp.int32, (c, c), 0)
    ci_ = lax.broadcasted_iota(jnp.int32, (c, c), 1)
    diff = (ri - ci_).astype(F32)
    dmask = jnp.where(diff >= 0, jnp.exp(jnp.maximum(diff, 0.0) * log_gamma), 0.0)
    idx = lax.broadcasted_iota(jnp.int32, (c, 1), 0).astype(F32)
    k_dec = jnp.exp((c - 1.0 - idx) * log_gamma)
    q_dec = jnp.exp((idx + 1.0) * log_gamma)
    chunk_decay = jnp.exp(c * log_gamma)

    qs, ks = [], []
    for b in range(nb):
        cos = cos_ref[b]
        sin = sin_ref[b]

        def rotary(x):
            return x * cos + pltpu.roll(x, LANES // 2, axis=1) * sin

        qs.append(rotary(q_ref[b]))
        ks.append(rotary(k_ref[b]) * (LANES ** -0.5))

    states = [state_ref[b] for b in range(nb)]
    for i in range(tt // c):
        sl = slice(i * c, (i + 1) * c)
        for b in range(nb):
            qc, kc, vc = qs[b][sl], ks[b][sl], v_ref[b, sl, :]
            scores = _dot_nt(qc, kc) * dmask
            y = _dot(scores, vc) + _dot(qc * q_dec, states[b])
            states[b] = states[b] * chunk_decay + _dot_tn(kc * k_dec, vc)
            mean = jnp.mean(y, axis=-1, keepdims=True)
            cen = y - mean
            var = jnp.mean(cen * cen, axis=-1, keepdims=True)
            yn = cen * lax.rsqrt(var + RET_GN_EPS) * gn_ref[...]
            out_ref[b, sl, :] = (_silu(g_ref[b, sl, :]) * yn).astype(out_ref.dtype)
    for b in range(nb):
        state_ref[b] = states[b]


def _ret_mix(pm, cos_t, sin_t, gn_w, tt):
    nb, seq, _ = pm.shape
    nt = seq // tt
    base = 4 * RWKV_PAIRS

    def col(off):
        return pl.BlockSpec((nb, tt, LANES), lambda h, i: (0, i, base + off + h))

    tab = pl.BlockSpec((nb, tt, LANES), lambda h, i: (0, i, 0))
    return pl.pallas_call(
        functools.partial(_ret_kernel, tt=tt, nb=nb),
        grid=(RET_HEADS, nt),
        in_specs=[col(0), col(RET_HEADS), col(2 * RET_HEADS), col(3 * RET_HEADS), tab, tab,
                  pl.BlockSpec((1, LANES), lambda h, i: (0, h))],
        out_specs=pl.BlockSpec((nb, tt, LANES), lambda h, i: (0, i, h)),
        out_shape=jax.ShapeDtypeStruct((nb, seq, RET_W), BF16),
        scratch_shapes=[pltpu.VMEM((nb, LANES, LANES), F32)],
        compiler_params=_cparams(("parallel", "arbitrary")),
        name="retention",
    )(pm, pm, pm, pm, cos_t, sin_t, gn_w.reshape(1, RET_W))


def _gdn_kernel(q_ref, k_ref, v_ref, z_ref, ab_ref, cw_ref, par_ref, nw_ref,
                out_ref, state_ref, carry_ref, *, tt, nb, nh):
    c = CHUNK
    nc = tt // c

    @pl.when(pl.program_id(1) == 0)
    def _():
        state_ref[...] = jnp.zeros_like(state_ref)
        carry_ref[...] = jnp.zeros_like(carry_ref)

    cw = cw_ref[...]

    def softplus(x):
        return jnp.maximum(x, 0.0) + jnp.log(1.0 + jnp.exp(-jnp.abs(x)))

    def l2norm(x):
        return x * lax.rsqrt(jnp.sum(x * x, axis=-1, keepdims=True) + L2_EPS)

    ri = lax.broadcasted_iota(jnp.int32, (c, c), 0)
    ci_ = lax.broadcasted_iota(jnp.int32, (c, c), 1)
    causal = ci_ <= ri
    strict = ci_ < ri
    eye = jnp.where(ci_ == ri, 1.0, 0.0)
    lane = lax.broadcasted_iota(jnp.int32, (1, LANES), 1)
    sub = lax.broadcasted_iota(jnp.int32, (2 * 8, 1), 0)

    rows = {}
    for b in range(nb):
        def conv_silu(ref, i):
            x = ref[b]
            prev = carry_ref[b, i]
            acc = x * cw[i, GDN_CONV - 1:GDN_CONV]
            for j in range(GDN_CONV - 1):
                acc = acc + _shift_rows(prev, x, GDN_CONV - 1 - j) * cw[i, j:j + 1]
            carry_ref[b, i] = x[tt - 8:]
            return _silu(acc)

        q_all = conv_silu(q_ref, 0)
        k_all = conv_silu(k_ref, 1)
        v_all = conv_silu(v_ref, 2)
        ab = ab_ref[b]
        ab_t = ab.T[:2 * 8]
        for hi in range(nh):
            ls = slice(hi * LANES, (hi + 1) * LANES)
            head = pl.program_id(0) * nh + hi
            par = par_ref[hi]
            neg_rate = -jnp.exp(par[:, 0:1])
            dt_bias = par[:, 1:2]
            a_col = jnp.sum(jnp.where(lane == head, ab, 0.0), axis=-1, keepdims=True)
            b_col = jnp.sum(jnp.where(lane == head + GDN_HEADS, ab, 0.0), axis=-1, keepdims=True)
            a_row = jnp.sum(jnp.where(sub == head, ab_t, 0.0), axis=0, keepdims=True)
            rows[b, hi] = dict(q=l2norm(q_all[:, ls]) * (LANES ** -0.5), k=l2norm(k_all[:, ls]),
                               v=v_all[:, ls],
                               g_col=neg_rate * softplus(a_col + dt_bias),
                               g_row=neg_rate * softplus(a_row + dt_bias),
                               beta=_sigmoid(b_col))

    keys = [(b, hi) for b in range(nb) for hi in range(nh)]
    items = [(ci,) + key for ci in range(nc) for key in keys]
    pre = {}
    for ci, b, hi in items:
        sl = slice(ci * c, (ci + 1) * c)
        rw = rows[b, hi]
        gr = rw["g_row"][:, sl]
        gcl = rw["g_col"][sl]
        gc_col = jnp.sum(jnp.where(causal, gr, 0.0), axis=1, keepdims=True)
        gc_row = jnp.sum(jnp.where(ri <= ci_, gcl, 0.0), axis=0, keepdims=True)
        g_last = gc_col[c - 1:c]
        decay = jnp.where(causal, jnp.exp(jnp.minimum(gc_col - gc_row, 0.0)), 0.0)
        qc, kc, vc, bc = rw["q"][sl], rw["k"][sl], rw["v"][sl], rw["beta"][sl]
        kb = kc * bc
        kk_t = _dot_nt(jnp.concatenate([kb, qc], axis=0), kc)
        e_gc = jnp.exp(gc_col)
        pre[ci, b, hi] = dict(
            lower=jnp.where(strict, kk_t[:c] * decay, 0.0),
            attn=kk_t[c:] * decay,
            vk=jnp.concatenate([vc * bc, kb * e_gc], axis=1),
            q_in=qc * e_gc,
            k_out=kc * jnp.exp(g_last - gc_col),
            p_end=jnp.exp(g_last))
    t_invs = _unit_lower_inverses([-pre[it]["lower"] for it in items], eye, lambda x: x, 5)
    for it, t_inv in zip(items, t_invs):
        pre[it]["uw"] = _dot(t_inv, pre[it]["vk"])

    states = {key: state_ref[key] for key in keys}
    for ci in range(nc):
        sl = slice(ci * c, (ci + 1) * c)
        for b, hi in keys:
            ls = slice(hi * LANES, (hi + 1) * LANES)
            p = pre[ci, b, hi]
            state = states[b, hi]
            v_new = p["uw"][:, :LANES] - _dot(p["uw"][:, LANES:], state)
            o = _dot(p["q_in"], state) + _dot(p["attn"], v_new)
            states[b, hi] = state * p["p_end"] + _dot_tn(p["k_out"], v_new)
            on = o * lax.rsqrt(jnp.mean(o * o, axis=-1, keepdims=True) + NORM_EPS)
            out_ref[b, sl, ls] = (on * nw_ref[:, ls] * _silu(z_ref[b, sl, ls])).astype(out_ref.dtype)
    for key in keys:
        state_ref[key] = states[key]


def _gdn_mix(pm, pe, conv_w, par, norm_w, tt):
    nb, seq, _ = pm.shape
    nt = seq // tt
    nh = 1
    ngrp = GDN_HEADS // nh
    wide = nh * LANES
    base = (4 * RWKV_PAIRS + 4 * RET_HEADS) // nh

    def col(off):
        return pl.BlockSpec((nb, tt, wide), lambda h, i: (0, i, base + off + h))

    return pl.pallas_call(
        functools.partial(_gdn_kernel, tt=tt, nb=nb, nh=nh),
        grid=(ngrp, nt),
        in_specs=[col(0), col(ngrp), col(2 * ngrp), col(3 * ngrp),
                  pl.BlockSpec((nb, tt, LANES), lambda h, i: (0, i, 6)),
                  pl.BlockSpec((3, GDN_CONV, wide), lambda h, i: (0, 0, h)),
                  pl.BlockSpec((nh, 1, LANES), lambda h, i: (h, 0, 0)),
                  pl.BlockSpec((1, wide), lambda h, i: (0, h))],
        out_specs=pl.BlockSpec((nb, tt, wide), lambda h, i: (0, i, h)),
        out_shape=jax.ShapeDtypeStruct((nb, seq, GDN_W), BF16),
        scratch_shapes=[pltpu.VMEM((nb, nh, LANES, LANES), F32), pltpu.VMEM((nb, 3, 8, wide), F32)],
        compiler_params=_cparams(("parallel", "arbitrary")),
        name="gated_deltanet",
    )(pm, pm, pm, pm, pe, conv_w, par, norm_w.reshape(1, GDN_W))


def _route(x, nw, wr, br):
    h = x * lax.rsqrt(jnp.mean(x * x, axis=-1, keepdims=True) + NORM_EPS) * nw
    h_hi = h.astype(BF16)
    h_lo = (h - h_hi.astype(F32)).astype(BF16)
    w_hi = wr.astype(BF16)
    w_lo = (wr - w_hi.astype(F32)).astype(BF16)
    logits = (jnp.dot(h_hi, w_hi, preferred_element_type=F32)
              + (jnp.dot(h_hi, w_lo, preferred_element_type=F32)
                 + jnp.dot(h_lo, w_hi, preferred_element_type=F32))) + br
    lane_i = lax.broadcasted_iota(jnp.int32, logits.shape, 1)
    lane = lane_i.astype(F32)
    group_of_lane = (lane_i // MOE_PER_GROUP).astype(F32)
    neg = -jnp.inf
    is_group = (lane_i >= MOE_EXPERTS) & (lane_i < MOE_EXPERTS + MOE_GROUPS)
    gl = jnp.where(is_group, logits, neg)
    gmax = jnp.max(gl, axis=-1, keepdims=True)
    gidx = jnp.min(jnp.where(gl == gmax, lane - MOE_EXPERTS, LANES), axis=-1, keepdims=True)
    group_gate = 1.0 / jnp.sum(jnp.exp(gl - gmax), axis=-1, keepdims=True)
    in_group = (lane_i < MOE_EXPERTS) & (group_of_lane == gidx)
    el = jnp.where(in_group, logits, neg)
    m1 = jnp.max(el, axis=-1, keepdims=True)
    i1 = jnp.min(jnp.where(el == m1, lane, LANES), axis=-1, keepdims=True)
    el2 = jnp.where(lane == i1, neg, el)
    m2 = jnp.max(el2, axis=-1, keepdims=True)
    i2 = jnp.min(jnp.where(el2 == m2, lane, LANES), axis=-1, keepdims=True)
    esum = jnp.sum(jnp.exp(el - m1), axis=-1, keepdims=True)
    p1 = 1.0 / esum
    p2 = jnp.exp(m2 - m1) / esum
    tot = p1 + p2
    within = jnp.where(lane == i1, p1 / tot, jnp.where(lane == i2, p2 / tot, 0.0))
    return h_hi, group_gate * within


def _moe_dense_kernel(h_ref, g_ref, w1_ref, w3_ref, w2_ref, x_ref, nw_ref, o_ref, *hn_ref, te):
    j = pl.program_id(1)

    @pl.when(j == 0)
    def _():
        o_ref[...] = x_ref[...]

    h = h_ref[...]
    gts = g_ref[...]
    lane = lax.broadcasted_iota(jnp.int32, (1, LANES), 1)
    acc = o_ref[...]
    for e in range(te):
        gate = jnp.sum(jnp.where(lane == j * te + e, gts, 0.0), axis=-1, keepdims=True)
        a1 = jnp.dot(h, w1_ref[e], preferred_element_type=F32)
        a3 = jnp.dot(h, w3_ref[e], preferred_element_type=F32)
        hid = _silu(a1) * a3 * gate
        acc = acc + jnp.dot(hid.astype(BF16), w2_ref[e], preferred_element_type=F32)
    o_ref[...] = acc

    @pl.when(j == pl.num_programs(1) - 1)
    def _():
        y = acc * lax.rsqrt(jnp.mean(acc * acc, axis=-1, keepdims=True) + NORM_EPS) * nw_ref[...]
        if hn_ref:
            hn_ref[0][...] = y.astype(hn_ref[0].dtype)
        else:
            o_ref[...] = y


def _moe_dense(h, gates, w1, w3, w2, x, norm_w, l, keep_residual, tm=512, te=4):
    t, d = x.shape
    ne = MOE_EXPERTS // te
    tile = pl.BlockSpec((tm, d), lambda i, j: (i, 0))
    out_specs = [tile, tile] if keep_residual else tile
    out_shape = jax.ShapeDtypeStruct((t, d), F32)
    if keep_residual:
        out_shape = [out_shape, jax.ShapeDtypeStruct((t, d), BF16)]
    return pl.pallas_call(
        functools.partial(_moe_dense_kernel, te=te),
        grid=(t // tm, ne),
        in_specs=[tile,
                  pl.BlockSpec((tm, LANES), lambda i, j: (i, 0)),
                  pl.BlockSpec((None, te, d, MOE_HIDDEN), lambda i, j: (l, j, 0, 0)),
                  pl.BlockSpec((None, te, d, MOE_HIDDEN), lambda i, j: (l, j, 0, 0)),
                  pl.BlockSpec((None, te, MOE_HIDDEN, d), lambda i, j: (l, j, 0, 0)),
                  tile, pl.BlockSpec((1, d), lambda i, j: (0, 0))],
        out_specs=out_specs,
        out_shape=out_shape,
        compiler_params=_cparams(("parallel", "arbitrary")),
        name="moe_experts",
    )(h, gates, w1, w3, w2, x, norm_w.reshape(1, d))


def _pad_cols(w, n):
    return jnp.pad(w, ((0, 0), (0, n - w.shape[1])))


def _pad_rows(w, n):
    return jnp.pad(w, ((0, n - w.shape[0]), (0, 0)))


def _mixers(l, h, batch, tt, cos_t, sin_t, v_first, wt_in, rwkv_mu_rkvg, rwkv_mu_wa, rwkv_w0, rwkv_w1,
            rwkv_w2, rwkv_a0, rwkv_a1, rwkv_a2, rwkv_k_k, rwkv_k_a, rwkv_r_k, rwkv_ln_w, rwkv_ln_b,
            rwkv_mu_vres, rwkv_v0, rwkv_v1, rwkv_v2, ret_gn_w, gdn_conv_w, gdn_a_log, gdn_dt_bias,
            gdn_norm_w):
    t = h.shape[0]
    seq = t // batch
    d = D_MODEL
    mu_w = rwkv_mu_wa[l][0][None, :]
    mu_a = rwkv_mu_wa[l][1][None, :]
    w1t, a1t = rwkv_w1[l].T, rwkv_a1[l].T
    blocks = [(1.0 - mu_w) * w1t, mu_w * w1t, (1.0 - mu_a) * a1t, mu_a * a1t]
    if l > 0:
        mu_v = rwkv_mu_vres[l - 1][None, :]
        v1t = rwkv_v1[l - 1].T
        blocks += [(1.0 - mu_v) * v1t, mu_v * v1t]
        v0 = rwkv_v0[l - 1]
        v2 = _pad_rows(rwkv_v2[l - 1], LANES).astype(BF16)
    else:
        blocks += [jnp.zeros((LANES, d), F32)] * 2
        v0 = jnp.zeros((RWKV_W,), F32)
        v2 = None
    blocks.append(wt_in[l, N_MAIN:])
    wt_ext = jnp.concatenate([_pad_rows(b.astype(BF16), LANES) for b in blocks], axis=0)

    pm = _matmul_nt(h, wt_in, N_MAIN, tm=min(1024, t), tn=1024, l=l).reshape(batch, seq, N_MAIN)
    pe = _matmul_nt(h, wt_ext, N_EXT, tm=min(1024, t), tn=N_EXT).reshape(batch, seq, N_EXT)

    par = jnp.stack([rwkv_w0[l], rwkv_a0[l], v0, rwkv_k_k[l], rwkv_k_a[l], rwkv_ln_w[l], rwkv_ln_b[l],
                     rwkv_r_k[l].reshape(RWKV_W)])
    out_a, v_first = _rwkv_mix(pm, pe, rwkv_mu_rkvg[l], par,
                               _pad_rows(rwkv_w2[l], LANES).astype(BF16),
                               _pad_rows(rwkv_a2[l], LANES).astype(BF16), v2, v_first, tt)
    out_b = _ret_mix(pm, cos_t, sin_t, ret_gn_w[l], tt)
    gpar = _pad_cols(jnp.stack([gdn_a_log[l], gdn_dt_bias[l]], axis=1), LANES).reshape(GDN_HEADS, 1, LANES)
    out_c = _gdn_mix(pm, pe, gdn_conv_w[l].reshape(GDN_CONV, 3, GDN_W).transpose(1, 0, 2), gpar,
                     gdn_norm_w[l], tt)
    return (out_a.reshape(t, RWKV_W), out_b.reshape(t, RET_W), out_c.reshape(t, GDN_W), v_first)


def kernel(x, positions, norm1_w, w_in, w_out, rwkv_mu_rkvg, rwkv_mu_wa, rwkv_w0, rwkv_w1, rwkv_w2, rwkv_a0, rwkv_a1, rwkv_a2, rwkv_k_k, rwkv_k_a, rwkv_r_k, rwkv_ln_w, rwkv_ln_b, rwkv_mu_vres, rwkv_v0, rwkv_v1, rwkv_v2, ret_gn_w, gdn_conv_w, gdn_a_log, gdn_dt_bias, gdn_norm_w, norm2_w, moe_group_w, moe_group_b, moe_expert_w, moe_expert_b, moe_w1, moe_w3, moe_w2, final_norm_w):
    batch, seq, d = x.shape
    t = batch * seq
    depth = w_in.shape[0]
    tt = min(256, seq)
    tm = min(512, t)
    xf = x.reshape(t, d)
    cos_t, sin_t = _rope_tables(positions, tm)
    cos_t = cos_t.reshape(batch, seq, LANES)
    sin_t = sin_t.reshape(batch, seq, LANES)
    w1_bf, w3_bf, w2_bf = moe_w1.astype(BF16), moe_w3.astype(BF16), moe_w2.astype(BF16)
    w_out_bf = w_out.astype(BF16)
    wt_in = jnp.transpose(w_in, (0, 2, 1)).astype(BF16)
    v_first = None
    h = _rmsnorm(xf, norm1_w[0], BF16, tm)
    for l in range(depth):
        out_a, out_b, out_c, v_first = _mixers(
            l, h, batch, tt, cos_t, sin_t, v_first, wt_in, rwkv_mu_rkvg, rwkv_mu_wa, rwkv_w0, rwkv_w1,
            rwkv_w2, rwkv_a0, rwkv_a1, rwkv_a2, rwkv_k_k, rwkv_k_a, rwkv_r_k, rwkv_ln_w, rwkv_ln_b,
            rwkv_mu_vres, rwkv_v0, rwkv_v1, rwkv_v2, ret_gn_w, gdn_conv_w, gdn_a_log, gdn_dt_bias,
            gdn_norm_w)
        wr = _pad_cols(jnp.concatenate([moe_expert_w[l], moe_group_w[l]], axis=1), LANES)
        br = _pad_cols(jnp.concatenate([moe_expert_b[l], moe_group_b[l]]).reshape(1, -1), LANES)
        xf, h2, gates = _outproj_route(out_a, out_b, out_c, w_out_bf, xf, norm2_w[l], wr, br, l, tm)
        if l + 1 < depth:
            xf, h = _moe_dense(h2, gates, w1_bf, w3_bf, w2_bf, xf, norm1_w[l + 1], l, True, tm)
        else:
            out = _moe_dense(h2, gates, w1_bf, w3_bf, w2_bf, xf, final_norm_w, l, False, tm)
    return out.reshape(batch, seq, d)
```

```python
import functools

import jax
import jax.numpy as jnp
from jax import lax
from jax.experimental import pallas as pl
from jax.experimental.pallas import tpu as pltpu

F32 = jnp.float32
BF16 = jnp.bfloat16

D_MODEL = 2048
RWKV_W = 768
RET_W = 512
GDN_W = 768
RWKV_HEAD = 64
RWKV_PAIRS = RWKV_W // 128
RET_HEADS = RET_W // 128
GDN_HEADS = GDN_W // 128
LANES = 128
RWKV_DECAY_SCALE = 0.6065306597126334
RWKV_GN_EPS = 64e-5
RET_GN_EPS = 1e-5
ROPE_BASE = 10000.0
GDN_CONV = 4
MOE_GROUPS = 4
MOE_PER_GROUP = 8
MOE_EXPERTS = 32
MOE_HIDDEN = 256
NORM_EPS = 1e-6
L2_EPS = 1e-6
N_MAIN = 4 * RWKV_W + 4 * RET_W + 4 * GDN_W
N_EXT = 7 * LANES
CHUNK = 64
RET_CHUNK = 128
VMEM_LIMIT = 56 * 1024 * 1024


def _cparams(sem):
    return pltpu.CompilerParams(dimension_semantics=sem, vmem_limit_bytes=VMEM_LIMIT)


def _dot(a, b):
    return jnp.dot(a.astype(BF16), b.astype(BF16), preferred_element_type=F32)


def _dot_nt(a, b):
    return lax.dot_general(a.astype(BF16), b.astype(BF16), (((1,), (1,)), ((), ())),
                           preferred_element_type=F32)


def _dot_tn(a, b):
    return lax.dot_general(a.astype(BF16), b.astype(BF16), (((0,), (0,)), ((), ())),
                           preferred_element_type=F32)


def _sigmoid(x):
    return 1.0 / (1.0 + jnp.exp(-x))


def _silu(x):
    return x * _sigmoid(x)


def _shift_rows(prev8, x, n):
    xs = jnp.concatenate([prev8, x], axis=0)
    return pltpu.roll(xs, n, axis=0)[8:]


def _cumsum_rows(x, tril):
    hi = x.astype(BF16)
    lo = (x - hi.astype(F32)).astype(BF16)
    return (jnp.dot(tril, hi, preferred_element_type=F32)
            + jnp.dot(tril, lo, preferred_element_type=F32))


def _unit_lower_inverses(lows, eye, expand, steps):
    xs = [eye + low for low in lows]
    ps = list(lows)
    for _ in range(steps):
        ps = [_dot(p, expand(p)) for p in ps]
        xs = [x + _dot(x, expand(p)) for x, p in zip(xs, ps)]
    return xs


def _rmsnorm_kernel(x_ref, w_ref, o_ref):
    x = x_ref[...]
    y = x * lax.rsqrt(jnp.mean(x * x, axis=-1, keepdims=True) + NORM_EPS) * w_ref[...]
    o_ref[...] = y.astype(o_ref.dtype)


def _rmsnorm(x, w, out_dtype, tm=512):
    t, d = x.shape
    return pl.pallas_call(
        _rmsnorm_kernel,
        grid=(t // tm,),
        in_specs=[pl.BlockSpec((tm, d), lambda i: (i, 0)), pl.BlockSpec((1, d), lambda i: (0, 0))],
        out_specs=pl.BlockSpec((tm, d), lambda i: (i, 0)),
        out_shape=jax.ShapeDtypeStruct((t, d), out_dtype),
        compiler_params=_cparams(("parallel",)),
        name="rmsnorm",
    )(x, w.reshape(1, d))


def _matmul_nt_kernel(a_ref, bt_ref, o_ref):
    o_ref[...] = lax.dot_general(a_ref[...], bt_ref[...], (((1,), (1,)), ((), ())),
                                 preferred_element_type=F32).astype(o_ref.dtype)


def _matmul_nt(a, bt, n, tm, tn, l=None, out_dtype=F32):
    m, k = a.shape
    if l is None:
        w_spec = pl.BlockSpec((tn, k), lambda j, i: (j, 0))
    else:
        w_spec = pl.BlockSpec((None, tn, k), lambda j, i: (l, j, 0))
    return pl.pallas_call(
        _matmul_nt_kernel,
        grid=(n // tn, m // tm),
        in_specs=[pl.BlockSpec((tm, k), lambda j, i: (i, 0)), w_spec],
        out_specs=pl.BlockSpec((tm, tn), lambda j, i: (i, j)),
        out_shape=jax.ShapeDtypeStruct((m, n), out_dtype),
        compiler_params=_cparams(("parallel", "parallel")),
        name="in_proj",
    )(a, bt)


def _outproj_kernel(a_ref, b_ref, c_ref, w_ref, x_ref, nw_ref, wr_ref, br_ref, o_ref, h_ref, gates_ref):
    acc = x_ref[...]
    acc = acc + jnp.dot(a_ref[...], w_ref[:RWKV_W], preferred_element_type=F32)
    acc = acc + jnp.dot(b_ref[...], w_ref[RWKV_W:RWKV_W + RET_W], preferred_element_type=F32)
    acc = acc + jnp.dot(c_ref[...], w_ref[RWKV_W + RET_W:], preferred_element_type=F32)
    o_ref[...] = acc
    h_ref[...], gates_ref[...] = _route(acc, nw_ref[...], wr_ref[...], br_ref[...])


def _outproj_route(oa, ob, oc, w_out, x, norm_w, wr, br, l, tm=512):
    t, d = x.shape
    row = lambda i: (i, 0)
    const = lambda i: (0, 0)
    return pl.pallas_call(
        _outproj_kernel,
        grid=(t // tm,),
        in_specs=[pl.BlockSpec((tm, RWKV_W), row), pl.BlockSpec((tm, RET_W), row),
                  pl.BlockSpec((tm, GDN_W), row),
                  pl.BlockSpec((None, d, d), lambda i: (l, 0, 0)), pl.BlockSpec((tm, d), row),
                  pl.BlockSpec((1, d), const), pl.BlockSpec((d, LANES), const),
                  pl.BlockSpec((1, LANES), const)],
        out_specs=[pl.BlockSpec((tm, d), row), pl.BlockSpec((tm, d), row),
                   pl.BlockSpec((tm, LANES), row)],
        out_shape=[jax.ShapeDtypeStruct((t, d), F32), jax.ShapeDtypeStruct((t, d), BF16),
                   jax.ShapeDtypeStruct((t, LANES), F32)],
        compiler_params=_cparams(("parallel",)),
        name="out_proj_route",
    )(oa, ob, oc, w_out, x, norm_w.reshape(1, d), wr, br)


def _rope_kernel(pos_ref, inv_ref, cos_ref, sin_ref):
    ang = pos_ref[...].astype(F32) * inv_ref[...]
    lane = lax.broadcasted_iota(jnp.int32, ang.shape, 1)
    s = jnp.sin(ang)
    cos_ref[...] = jnp.cos(ang)
    sin_ref[...] = jnp.where(lane < LANES // 2, -s, s)


def _rope_tables(positions, tm=512):
    t = positions.size
    inv = ROPE_BASE ** (-jnp.arange(0, LANES, 2, dtype=F32) / LANES)
    inv = jnp.concatenate([inv, inv]).reshape(1, LANES)
    tm = min(tm, t)
    return pl.pallas_call(
        _rope_kernel,
        grid=(t // tm,),
        in_specs=[pl.BlockSpec((tm, 1), lambda i: (i, 0)), pl.BlockSpec((1, LANES), lambda i: (0, 0))],
        out_specs=[pl.BlockSpec((tm, LANES), lambda i: (i, 0))] * 2,
        out_shape=[jax.ShapeDtypeStruct((t, LANES), F32)] * 2,
        compiler_params=_cparams(("parallel",)),
        name="rope_tables",
    )(positions.reshape(t, 1), inv)


def _rwkv_kernel(*refs, tt, nb, npp, has_vres):
    if has_vres:
        (r_ref, k_ref, v_ref, g_ref, ext_ref, mu_ref, par_ref, w2_ref, a2_ref, v2_ref, vf_ref,
         out_ref, state_ref, carry_ref, ecarry_ref) = refs
    else:
        (r_ref, k_ref, v_ref, g_ref, ext_ref, mu_ref, par_ref, w2_ref, a2_ref,
         out_ref, vf_out_ref, state_ref, carry_ref, ecarry_ref) = refs
    c = CHUNK
    hd = RWKV_HEAD
    nc = tt // c

    @pl.when(pl.program_id(1) == 0)
    def _():
        state_ref[...] = jnp.zeros_like(state_ref)
        carry_ref[...] = jnp.zeros_like(carry_ref)
        ecarry_ref[...] = jnp.zeros_like(ecarry_ref)

    lane = lax.broadcasted_iota(jnp.int32, (1, LANES), 1)
    first = lane < hd

    def head_sum(x):
        s1 = jnp.sum(jnp.where(first, x, 0.0), axis=-1, keepdims=True)
        s2 = jnp.sum(jnp.where(first, 0.0, x), axis=-1, keepdims=True)
        return jnp.where(first, s1, s2)

    def expand(x):
        return jnp.concatenate([jnp.where(first, x, 0.0), jnp.where(first, 0.0, x)], axis=0)

    mu = mu_ref[...]
    par = par_ref[...]

    row = lax.broadcasted_iota(jnp.int32, (c, 2 * c), 0)
    col = lax.broadcasted_iota(jnp.int32, (c, 2 * c), 1) & (c - 1)
    strict = col < row
    incl = col <= row
    eye = jnp.where(col == row, 1.0, 0.0)
    tri_r = lax.broadcasted_iota(jnp.int32, (c, c), 0)
    tri_c = lax.broadcasted_iota(jnp.int32, (c, c), 1)
    tril = jnp.where(tri_c <= tri_r, 1.0, 0.0).astype(BF16)
    srow = lax.broadcasted_iota(jnp.int32, (LANES, LANES), 0)
    scol = lax.broadcasted_iota(jnp.int32, (LANES, LANES), 1)
    same_head = (srow < hd) == (scol < hd)

    rows = {}
    for b in range(nb):
        def mixed(ref, i):
            x = ref[b].astype(F32)
            xs = _shift_rows(carry_ref[b, i], x, 1)
            carry_ref[b, i] = x[tt - 8:]
            return x + (xs - x) * mu[i:i + 1]

        r_all = mixed(r_ref, 0)
        k_all = mixed(k_ref, 1)
        v_all = mixed(v_ref, 2)
        g_all = mixed(g_ref, 3)
        ext = ext_ref[b]
        exts = _shift_rows(ecarry_ref[b], ext, 1)
        ecarry_ref[b] = ext[tt - 8:]

        def low_rank(i):
            return (ext[:, 2 * i * LANES:(2 * i + 1) * LANES]
                    + exts[:, (2 * i + 1) * LANES:(2 * i + 2) * LANES])

        lr_w = jnp.tanh(low_rank(0))
        lr_a = low_rank(1)
        lr_v = low_rank(2) if has_vres else None
        for pi in range(npp):
            ls = slice(pi * LANES, (pi + 1) * LANES)
            w0, a0, v0, k_k, k_a, ln_w, ln_b, r_k = (par[i:i + 1, ls] for i in range(8))
            r, k, v, g = r_all[:, ls], k_all[:, ls], v_all[:, ls], g_all[:, ls]
            log_w = -RWKV_DECAY_SCALE * _sigmoid(w0 + _dot(lr_w, w2_ref[:, ls]))
            a = _sigmoid(a0 + _dot(lr_a, a2_ref[:, ls]))
            if has_vres:
                v = v + (vf_ref[b, :, ls] - v) * _sigmoid(v0 + _dot(lr_v, v2_ref[:, ls]))
            else:
                vf_out_ref[b, :, ls] = v
            kk = k * k_k
            kk = kk * lax.rsqrt(head_sum(kk * kk) + L2_EPS)
            k = k * (1.0 + (a - 1.0) * k_a)
            bonus = head_sum(r * k * r_k) * v
            rows[b, pi] = dict(r=r, k=k, v=v, g=g, log_w=log_w, kk=kk, kka=kk * a, bonus=bonus,
                               ln_w=ln_w, ln_b=ln_b)

    items = [(ci, b, pi) for ci in range(nc) for b in range(nb) for pi in range(npp)]
    pre = {}
    for ci, b, pi in items:
        sl = slice(ci * c, (ci + 1) * c)
        rw = rows[b, pi]
        lw = rw["log_w"][sl]
        gc = _cumsum_rows(lw, tril)
        g_last = gc[c - 1:c]
        dec_inv = jnp.exp(-gc)
        dec_out = jnp.exp(g_last - gc)
        a_t = -rw["kk"][sl] * jnp.exp(gc - lw)
        r_t = rw["r"][sl] * jnp.exp(gc)
        b_t = rw["kka"][sl] * dec_inv
        k_t = rw["k"][sl] * dec_inv
        sc = _dot_nt(jnp.concatenate([a_t, r_t], axis=0),
                     jnp.concatenate([expand(b_t), expand(k_t)], axis=0))
        pre[ci, b, pi] = dict(
            a_t=a_t, r_t=r_t, v=rw["v"][sl],
            l_ab=jnp.where(strict, sc[:c, :2 * c], 0.0),
            l_ak=jnp.where(strict, sc[:c, 2 * c:], 0.0),
            m_rb=jnp.where(incl, sc[c:, :2 * c], 0.0),
            m_rk=jnp.where(incl, sc[c:, 2 * c:], 0.0),
            kd=jnp.concatenate([rw["kka"][sl] * dec_out, rw["k"][sl] * dec_out], axis=0),
            p_end=jnp.exp(g_last))
    t_invs = _unit_lower_inverses([pre[it]["l_ab"] for it in items], eye, expand, 5)
    for it, t_inv in zip(items, t_invs):
        p = pre[it]
        ev = expand(p["v"])
        wu = _dot(t_inv, jnp.concatenate([expand(p["a_t"]), expand(_dot(p["l_ak"], ev))], axis=1))
        p["w_t"] = wu[:, :LANES]
        p["u0"] = wu[:, LANES:]
        p["o0"] = _dot(p["m_rk"], ev)

    keys = [(b, pi) for b in range(nb) for pi in range(npp)]
    states = {key: state_ref[key] for key in keys}
    for ci in range(nc):
        sl = slice(ci * c, (ci + 1) * c)
        for b, pi in keys:
            p = pre[ci, b, pi]
            rw = rows[b, pi]
            state = states[b, pi]
            u = p["u0"] + _dot_nt(p["w_t"], state)
            o = p["o0"] + _dot_nt(p["r_t"], state) + _dot(p["m_rb"], expand(u))
            upd = _dot_tn(jnp.concatenate([u, p["v"]], axis=0), p["kd"])
            states[b, pi] = state * p["p_end"] + jnp.where(same_head, upd, 0.0)

            mean = head_sum(o) * (1.0 / hd)
            cen = o - mean
            var = head_sum(cen * cen) * (1.0 / hd)
            y = cen * lax.rsqrt(var + RWKV_GN_EPS) * rw["ln_w"] + rw["ln_b"]
            out_ref[b, sl, pi * LANES:(pi + 1) * LANES] = (
                (y + rw["bonus"][sl]) * _sigmoid(rw["g"][sl])).astype(out_ref.dtype)
    for key in keys:
        state_ref[key] = states[key]


def _rwkv_mix(pm, pe, mu_rkvg, par, w2, a2, v2, v_first, tt):
    nb, seq, _ = pm.shape
    nt = seq // tt
    has_vres = v_first is not None
    npp = 2
    ngrp = RWKV_PAIRS // npp
    wide = npp * LANES

    def col(off):
        return pl.BlockSpec((nb, tt, wide), lambda p, i: (0, i, off + p))

    pcol = lambda nrows: pl.BlockSpec((nrows, wide), lambda p, i: (0, p))
    in_specs = [col(0), col(ngrp), col(2 * ngrp), col(3 * ngrp),
                pl.BlockSpec((nb, tt, 6 * LANES), lambda p, i: (0, i, 0)),
                pcol(4), pcol(8), pcol(LANES), pcol(LANES)]
    args = [pm, pm, pm, pm, pe, mu_rkvg, par, w2, a2]
    out_block = pl.BlockSpec((nb, tt, wide), lambda p, i: (0, i, p))
    if has_vres:
        in_specs += [pcol(LANES), out_block]
        args += [v2, v_first]
        out_specs = out_block
        out_shape = jax.ShapeDtypeStruct((nb, seq, RWKV_W), BF16)
    else:
        out_specs = [out_block, out_block]
        out_shape = [jax.ShapeDtypeStruct((nb, seq, RWKV_W), BF16),
                     jax.ShapeDtypeStruct((nb, seq, RWKV_W), F32)]
    res = pl.pallas_call(
        functools.partial(_rwkv_kernel, tt=tt, nb=nb, npp=npp, has_vres=has_vres),
        grid=(ngrp, nt),
        in_specs=in_specs,
        out_specs=out_specs,
        out_shape=out_shape,
        scratch_shapes=[pltpu.VMEM((nb, npp, LANES, LANES), F32), pltpu.VMEM((nb, 4, 8, wide), F32),
                        pltpu.VMEM((nb, 8, 6 * LANES), F32)],
        compiler_params=_cparams(("parallel", "arbitrary")),
        name="rwkv7_vres" if has_vres else "rwkv7",
    )(*args)
    if has_vres:
        return res, v_first
    return res[0], res[1]


def _ret_kernel(q_ref, k_ref, v_ref, g_ref, cos_ref, sin_ref, gn_ref, out_ref, state_ref, *, tt, nb):
    c = RET_CHUNK

    @pl.when(pl.program_id(1) == 0)
    def _():
        state_ref[...] = jnp.zeros_like(state_ref)

    head = pl.program_id(0).astype(F32)
    log_gamma = jnp.log(1.0 - jnp.exp2(jnp.full((1, 1), -5.0, F32) - head))
    ri = lax.broadcasted_iota(jnp.int32, (c, c), 0)
    ci_ = lax.broadcasted_iota(jnp.int32, (c, c), 1)
    diff = (ri - ci_).astype(F32)
    dmask = jnp.where(diff >= 0, jnp.exp(jnp.maximum(diff, 0.0) * log_gamma), 0.0)
    idx = lax.broadcasted_iota(jnp.int32, (c, 1), 0).astype(F32)
    k_dec = jnp.exp((c - 1.0 - idx) * log_gamma)
    q_dec = jnp.exp((idx + 1.0) * log_gamma)
    chunk_decay = jnp.exp(c * log_gamma)

    qs, ks = [], []
    for b in range(nb):
        cos = cos_ref[b]
        sin = sin_ref[b]

        def rotary(x):
            return x * cos + pltpu.roll(x, LANES // 2, axis=1) * sin

        qs.append(rotary(q_ref[b].astype(F32)))
        ks.append(rotary(k_ref[b].astype(F32)) * (LANES ** -0.5))

    states = [state_ref[b] for b in range(nb)]
    for i in range(tt // c):
        sl = slice(i * c, (i + 1) * c)
        for b in range(nb):
            qc, kc, vc = qs[b][sl], ks[b][sl], v_ref[b, sl, :]
            scores = _dot_nt(qc, kc) * dmask
            y = _dot(scores, vc) + _dot(qc * q_dec, states[b])
            states[b] = states[b] * chunk_decay + _dot_tn(kc * k_dec, vc)
            mean = jnp.mean(y, axis=-1, keepdims=True)
            cen = y - mean
            var = jnp.mean(cen * cen, axis=-1, keepdims=True)
            yn = cen * lax.rsqrt(var + RET_GN_EPS) * gn_ref[...]
            out_ref[b, sl, :] = (_silu(g_ref[b, sl, :].astype(F32)) * yn).astype(out_ref.dtype)
    for b in range(nb):
        state_ref[b] = states[b]


def _ret_mix(pm, cos_t, sin_t, gn_w, tt):
    nb, seq, _ = pm.shape
    nt = seq // tt
    base = 4 * RWKV_PAIRS

    def col(off):
        return pl.BlockSpec((nb, tt, LANES), lambda h, i: (0, i, base + off + h))

    tab = pl.BlockSpec((nb, tt, LANES), lambda h, i: (0, i, 0))
    return pl.pallas_call(
        functools.partial(_ret_kernel, tt=tt, nb=nb),
        grid=(RET_HEADS, nt),
        in_specs=[col(0), col(RET_HEADS), col(2 * RET_HEADS), col(3 * RET_HEADS), tab, tab,
                  pl.BlockSpec((1, LANES), lambda h, i: (0, h))],
        out_specs=pl.BlockSpec((nb, tt, LANES), lambda h, i: (0, i, h)),
        out_shape=jax.ShapeDtypeStruct((nb, seq, RET_W), BF16),
        scratch_shapes=[pltpu.VMEM((nb, LANES, LANES), F32)],
        compiler_params=_cparams(("parallel", "arbitrary")),
        name="retention",
    )(pm, pm, pm, pm, cos_t, sin_t, gn_w.reshape(1, RET_W))


def _gdn_kernel(q_ref, k_ref, v_ref, z_ref, ab_ref, cw_ref, par_ref, nw_ref,
                out_ref, state_ref, carry_ref, *, tt, nb, nh):
    c = CHUNK
    nc = tt // c

    @pl.when(pl.program_id(1) == 0)
    def _():
        state_ref[...] = jnp.zeros_like(state_ref)
        carry_ref[...] = jnp.zeros_like(carry_ref)

    cw = cw_ref[...]

    def softplus(x):
        return jnp.maximum(x, 0.0) + jnp.log(1.0 + jnp.exp(-jnp.abs(x)))

    def l2norm(x):
        return x * lax.rsqrt(jnp.sum(x * x, axis=-1, keepdims=True) + L2_EPS)

    ri = lax.broadcasted_iota(jnp.int32, (c, c), 0)
    ci_ = lax.broadcasted_iota(jnp.int32, (c, c), 1)
    causal = ci_ <= ri
    strict = ci_ < ri
    eye = jnp.where(ci_ == ri, 1.0, 0.0)
    lane = lax.broadcasted_iota(jnp.int32, (1, LANES), 1)
    sub = lax.broadcasted_iota(jnp.int32, (2 * 8, 1), 0)

    rows = {}
    for b in range(nb):
        def conv_silu(ref, i):
            x = ref[b].astype(F32)
            prev = carry_ref[b, i]
            acc = x * cw[i, GDN_CONV - 1:GDN_CONV]
            for j in range(GDN_CONV - 1):
                acc = acc + _shift_rows(prev, x, GDN_CONV - 1 - j) * cw[i, j:j + 1]
            carry_ref[b, i] = x[tt - 8:]
            return _silu(acc)

        q_all = conv_silu(q_ref, 0)
        k_all = conv_silu(k_ref, 1)
        v_all = conv_silu(v_ref, 2)
        ab = ab_ref[b]
        ab_t = ab.T[:2 * 8]
        for hi in range(nh):
            ls = slice(hi * LANES, (hi + 1) * LANES)
            head = pl.program_id(0) * nh + hi
            par = par_ref[hi]
            neg_rate = -jnp.exp(par[:, 0:1])
            dt_bias = par[:, 1:2]
            a_col = jnp.sum(jnp.where(lane == head, ab, 0.0), axis=-1, keepdims=True)
            b_col = jnp.sum(jnp.where(lane == head + GDN_HEADS, ab, 0.0), axis=-1, keepdims=True)
            a_row = jnp.sum(jnp.where(sub == head, ab_t, 0.0), axis=0, keepdims=True)
            rows[b, hi] = dict(q=l2norm(q_all[:, ls]) * (LANES ** -0.5), k=l2norm(k_all[:, ls]),
                               v=v_all[:, ls],
                               g_col=neg_rate * softplus(a_col + dt_bias),
                               g_row=neg_rate * softplus(a_row + dt_bias),
                               beta=_sigmoid(b_col))

    keys = [(b, hi) for b in range(nb) for hi in range(nh)]
    items = [(ci,) + key for ci in range(nc) for key in keys]
    pre = {}
    for ci, b, hi in items:
        sl = slice(ci * c, (ci + 1) * c)
        rw = rows[b, hi]
        gr = rw["g_row"][:, sl]
        gcl = rw["g_col"][sl]
        gc_col = jnp.sum(jnp.where(causal, gr, 0.0), axis=1, keepdims=True)
        gc_row = jnp.sum(jnp.where(ri <= ci_, gcl, 0.0), axis=0, keepdims=True)
        g_last = gc_col[c - 1:c]
        decay = jnp.where(causal, jnp.exp(jnp.minimum(gc_col - gc_row, 0.0)), 0.0)
        qc, kc, vc, bc = rw["q"][sl], rw["k"][sl], rw["v"][sl], rw["beta"][sl]
        kb = kc * bc
        kk_t = _dot_nt(jnp.concatenate([kb, qc], axis=0), kc)
        e_gc = jnp.exp(gc_col)
        pre[ci, b, hi] = dict(
            lower=jnp.where(strict, kk_t[:c] * decay, 0.0),
            attn=kk_t[c:] * decay,
            vk=jnp.concatenate([vc * bc, kb * e_gc], axis=1),
            q_in=qc * e_gc,
            k_out=kc * jnp.exp(g_last - gc_col),
            p_end=jnp.exp(g_last))
    t_invs = _unit_lower_inverses([-pre[it]["lower"] for it in items], eye, lambda x: x, 5)
    for it, t_inv in zip(items, t_invs):
        pre[it]["uw"] = _dot(t_inv, pre[it]["vk"])

    states = {key: state_ref[key] for key in keys}
    for ci in range(nc):
        sl = slice(ci * c, (ci + 1) * c)
        for b, hi in keys:
            ls = slice(hi * LANES, (hi + 1) * LANES)
            p = pre[ci, b, hi]
            state = states[b, hi]
            v_new = p["uw"][:, :LANES] - _dot(p["uw"][:, LANES:], state)
            o = _dot(p["q_in"], state) + _dot(p["attn"], v_new)
            states[b, hi] = state * p["p_end"] + _dot_tn(p["k_out"], v_new)
            on = o * lax.rsqrt(jnp.mean(o * o, axis=-1, keepdims=True) + NORM_EPS)
            out_ref[b, sl, ls] = (on * nw_ref[:, ls] * _silu(z_ref[b, sl, ls].astype(F32))).astype(out_ref.dtype)
    for key in keys:
        state_ref[key] = states[key]


def _gdn_mix(pm, pe, conv_w, par, norm_w, tt):
    nb, seq, _ = pm.shape
    nt = seq // tt
    nh = 1
    ngrp = GDN_HEADS // nh
    wide = nh * LANES
    base = (4 * RWKV_PAIRS + 4 * RET_HEADS) // nh

    def col(off):
        return pl.BlockSpec((nb, tt, wide), lambda h, i: (0, i, base + off + h))

    return pl.pallas_call(
        functools.partial(_gdn_kernel, tt=tt, nb=nb, nh=nh),
        grid=(ngrp, nt),
        in_specs=[col(0), col(ngrp), col(2 * ngrp), col(3 * ngrp),
                  pl.BlockSpec((nb, tt, LANES), lambda h, i: (0, i, 6)),
                  pl.BlockSpec((3, GDN_CONV, wide), lambda h, i: (0, 0, h)),
                  pl.BlockSpec((nh, 1, LANES), lambda h, i: (h, 0, 0)),
                  pl.BlockSpec((1, wide), lambda h, i: (0, h))],
        out_specs=pl.BlockSpec((nb, tt, wide), lambda h, i: (0, i, h)),
        out_shape=jax.ShapeDtypeStruct((nb, seq, GDN_W), BF16),
        scratch_shapes=[pltpu.VMEM((nb, nh, LANES, LANES), F32), pltpu.VMEM((nb, 3, 8, wide), F32)],
        compiler_params=_cparams(("parallel", "arbitrary")),
        name="gated_deltanet",
    )(pm, pm, pm, pm, pe, conv_w, par, norm_w.reshape(1, GDN_W))


def _route(x, nw, wr, br):
    h = x * lax.rsqrt(jnp.mean(x * x, axis=-1, keepdims=True) + NORM_EPS) * nw
    h_hi = h.astype(BF16)
    h_lo = (h - h_hi.astype(F32)).astype(BF16)
    w_hi = wr.astype(BF16)
    w_lo = (wr - w_hi.astype(F32)).astype(BF16)
    logits = (jnp.dot(h_hi, w_hi, preferred_element_type=F32)
              + (jnp.dot(h_hi, w_lo, preferred_element_type=F32)
                 + jnp.dot(h_lo, w_hi, preferred_element_type=F32))) + br
    lane_i = lax.broadcasted_iota(jnp.int32, logits.shape, 1)
    lane = lane_i.astype(F32)
    group_of_lane = (lane_i // MOE_PER_GROUP).astype(F32)
    neg = -jnp.inf
    is_group = (lane_i >= MOE_EXPERTS) & (lane_i < MOE_EXPERTS + MOE_GROUPS)
    gl = jnp.where(is_group, logits, neg)
    gmax = jnp.max(gl, axis=-1, keepdims=True)
    gidx = jnp.min(jnp.where(gl == gmax, lane - MOE_EXPERTS, LANES), axis=-1, keepdims=True)
    group_gate = 1.0 / jnp.sum(jnp.exp(gl - gmax), axis=-1, keepdims=True)
    in_group = (lane_i < MOE_EXPERTS) & (group_of_lane == gidx)
    el = jnp.where(in_group, logits, neg)
    m1 = jnp.max(el, axis=-1, keepdims=True)
    i1 = jnp.min(jnp.where(el == m1, lane, LANES), axis=-1, keepdims=True)
    el2 = jnp.where(lane == i1, neg, el)
    m2 = jnp.max(el2, axis=-1, keepdims=True)
    i2 = jnp.min(jnp.where(el2 == m2, lane, LANES), axis=-1, keepdims=True)
    esum = jnp.sum(jnp.exp(el - m1), axis=-1, keepdims=True)
    p1 = 1.0 / esum
    p2 = jnp.exp(m2 - m1) / esum
    tot = p1 + p2
    within = jnp.where(lane == i1, p1 / tot, jnp.where(lane == i2, p2 / tot, 0.0))
    return h_hi, group_gate * within


def _moe_dense_kernel(h_ref, g_ref, w1_ref, w3_ref, w2_ref, x_ref, nw_ref, o_ref, *hn_ref, te):
    j = pl.program_id(1)

    @pl.when(j == 0)
    def _():
        o_ref[...] = x_ref[...]

    h = h_ref[...]
    gts = g_ref[...]
    lane = lax.broadcasted_iota(jnp.int32, (1, LANES), 1)
    acc = o_ref[...]
    for e in range(te):
        gate = jnp.sum(jnp.where(lane == j * te + e, gts, 0.0), axis=-1, keepdims=True)
        a1 = jnp.dot(h, w1_ref[e], preferred_element_type=F32)
        a3 = jnp.dot(h, w3_ref[e], preferred_element_type=F32)
        hid = _silu(a1) * a3 * gate
        acc = acc + jnp.dot(hid.astype(BF16), w2_ref[e], preferred_element_type=F32)
    o_ref[...] = acc

    @pl.when(j == pl.num_programs(1) - 1)
    def _():
        y = acc * lax.rsqrt(jnp.mean(acc * acc, axis=-1, keepdims=True) + NORM_EPS) * nw_ref[...]
        if hn_ref:
            hn_ref[0][...] = y.astype(hn_ref[0].dtype)
        else:
            o_ref[...] = y


def _moe_dense(h, gates, w1, w3, w2, x, norm_w, l, keep_residual, tm=512, te=4):
    t, d = x.shape
    ne = MOE_EXPERTS // te
    tile = pl.BlockSpec((tm, d), lambda i, j: (i, 0))
    out_specs = [tile, tile] if keep_residual else tile
    out_shape = jax.ShapeDtypeStruct((t, d), F32)
    if keep_residual:
        out_shape = [out_shape, jax.ShapeDtypeStruct((t, d), BF16)]
    return pl.pallas_call(
        functools.partial(_moe_dense_kernel, te=te),
        grid=(t // tm, ne),
        in_specs=[tile,
                  pl.BlockSpec((tm, LANES), lambda i, j: (i, 0)),
                  pl.BlockSpec((None, te, d, MOE_HIDDEN), lambda i, j: (l, j, 0, 0)),
                  pl.BlockSpec((None, te, d, MOE_HIDDEN), lambda i, j: (l, j, 0, 0)),
                  pl.BlockSpec((None, te, MOE_HIDDEN, d), lambda i, j: (l, j, 0, 0)),
                  tile, pl.BlockSpec((1, d), lambda i, j: (0, 0))],
        out_specs=out_specs,
        out_shape=out_shape,
        compiler_params=_cparams(("parallel", "arbitrary")),
        name="moe_experts",
    )(h, gates, w1, w3, w2, x, norm_w.reshape(1, d))


def _pad_cols(w, n):
    return jnp.pad(w, ((0, 0), (0, n - w.shape[1])))


def _pad_rows(w, n):
    return jnp.pad(w, ((0, n - w.shape[0]), (0, 0)))


def _mixers(l, h, batch, tt, cos_t, sin_t, v_first, wt_in, rwkv_mu_rkvg, rwkv_mu_wa, rwkv_w0, rwkv_w1,
            rwkv_w2, rwkv_a0, rwkv_a1, rwkv_a2, rwkv_k_k, rwkv_k_a, rwkv_r_k, rwkv_ln_w, rwkv_ln_b,
            rwkv_mu_vres, rwkv_v0, rwkv_v1, rwkv_v2, ret_gn_w, gdn_conv_w, gdn_a_log, gdn_dt_bias,
            gdn_norm_w):
    t = h.shape[0]
    seq = t // batch
    d = D_MODEL
    mu_w = rwkv_mu_wa[l][0][None, :]
    mu_a = rwkv_mu_wa[l][1][None, :]
    w1t, a1t = rwkv_w1[l].T, rwkv_a1[l].T
    blocks = [(1.0 - mu_w) * w1t, mu_w * w1t, (1.0 - mu_a) * a1t, mu_a * a1t]
    if l > 0:
        mu_v = rwkv_mu_vres[l - 1][None, :]
        v1t = rwkv_v1[l - 1].T
        blocks += [(1.0 - mu_v) * v1t, mu_v * v1t]
        v0 = rwkv_v0[l - 1]
        v2 = _pad_rows(rwkv_v2[l - 1], LANES).astype(BF16)
    else:
        blocks += [jnp.zeros((LANES, d), F32)] * 2
        v0 = jnp.zeros((RWKV_W,), F32)
        v2 = None
    blocks.append(wt_in[l, N_MAIN:])
    wt_ext = jnp.concatenate([_pad_rows(b.astype(BF16), LANES) for b in blocks], axis=0)

    pm = _matmul_nt(h, wt_in, N_MAIN, tm=min(1024, t), tn=1024, l=l, out_dtype=BF16).reshape(batch, seq, N_MAIN)
    pe = _matmul_nt(h, wt_ext, N_EXT, tm=min(1024, t), tn=N_EXT).reshape(batch, seq, N_EXT)

    par = jnp.stack([rwkv_w0[l], rwkv_a0[l], v0, rwkv_k_k[l], rwkv_k_a[l], rwkv_ln_w[l], rwkv_ln_b[l],
                     rwkv_r_k[l].reshape(RWKV_W)])
    out_a, v_first = _rwkv_mix(pm, pe, rwkv_mu_rkvg[l], par,
                               _pad_rows(rwkv_w2[l], LANES).astype(BF16),
                               _pad_rows(rwkv_a2[l], LANES).astype(BF16), v2, v_first, tt)
    out_b = _ret_mix(pm, cos_t, sin_t, ret_gn_w[l], tt)
    gpar = _pad_cols(jnp.stack([gdn_a_log[l], gdn_dt_bias[l]], axis=1), LANES).reshape(GDN_HEADS, 1, LANES)
    out_c = _gdn_mix(pm, pe, gdn_conv_w[l].reshape(GDN_CONV, 3, GDN_W).transpose(1, 0, 2), gpar,
                     gdn_norm_w[l], tt)
    return (out_a.reshape(t, RWKV_W), out_b.reshape(t, RET_W), out_c.reshape(t, GDN_W), v_first)


def kernel(x, positions, norm1_w, w_in, w_out, rwkv_mu_rkvg, rwkv_mu_wa, rwkv_w0, rwkv_w1, rwkv_w2, rwkv_a0, rwkv_a1, rwkv_a2, rwkv_k_k, rwkv_k_a, rwkv_r_k, rwkv_ln_w, rwkv_ln_b, rwkv_mu_vres, rwkv_v0, rwkv_v1, rwkv_v2, ret_gn_w, gdn_conv_w, gdn_a_log, gdn_dt_bias, gdn_norm_w, norm2_w, moe_group_w, moe_group_b, moe_expert_w, moe_expert_b, moe_w1, moe_w3, moe_w2, final_norm_w):
    batch, seq, d = x.shape
    t = batch * seq
    depth = w_in.shape[0]
    tt = min(256, seq)
    tm = min(512, t)
    xf = x.reshape(t, d)
    cos_t, sin_t = _rope_tables(positions, tm)
    cos_t = cos_t.reshape(batch, seq, LANES)
    sin_t = sin_t.reshape(batch, seq, LANES)
    w1_bf, w3_bf, w2_bf = moe_w1.astype(BF16), moe_w3.astype(BF16), moe_w2.astype(BF16)
    w_out_bf = w_out.astype(BF16)
    wt_in = jnp.transpose(w_in, (0, 2, 1)).astype(BF16)
    v_first = None
    h = _rmsnorm(xf, norm1_w[0], BF16, tm)
    for l in range(depth):
        out_a, out_b, out_c, v_first = _mixers(
            l, h, batch, tt, cos_t, sin_t, v_first, wt_in, rwkv_mu_rkvg, rwkv_mu_wa, rwkv_w0, rwkv_w1,
            rwkv_w2, rwkv_a0, rwkv_a1, rwkv_a2, rwkv_k_k, rwkv_k_a, rwkv_r_k, rwkv_ln_w, rwkv_ln_b,
            rwkv_mu_vres, rwkv_v0, rwkv_v1, rwkv_v2, ret_gn_w, gdn_conv_w, gdn_a_log, gdn_dt_bias,
            gdn_norm_w)
        wr = _pad_cols(jnp.concatenate([moe_expert_w[l], moe_group_w[l]], axis=1), LANES)
        br = _pad_cols(jnp.concatenate([moe_expert_b[l], moe_group_b[l]]).reshape(1, -1), LANES)
        xf, h2, gates = _outproj_route(out_a, out_b, out_c, w_out_bf, xf, norm2_w[l], wr, br, l, tm)
        if l + 1 < depth:
            xf, h = _moe_dense(h2, gates, w1_bf, w3_bf, w2_bf, xf, norm1_w[l + 1], l, True, tm)
        else:
            out = _moe_dense(h2, gates, w1_bf, w3_bf, w2_bf, xf, final_norm_w, l, False, tm)
    return out.reshape(batch, seq, d)
```
